```python
import math
import jax
import jax.numpy as jnp
from jax import lax
import numpy as np

D_MODEL = 2048
BATCH = 8
SEQ = 4096
DEPTH = 4

CTX_LEN = 256
GRID_W = 64
EPS = 1e-6
N_MOD = 6
N_BRANCH = 3

DA_HEADS = 4
DA_DIM = 128
DA_WIDTH = DA_HEADS * 2 * DA_DIM
DA_SCALE = DA_DIM ** -0.5
Q_BLOCK = 128
ROPE_BASE = 10000.0

RNN_WIDTH = 1024
RNN_BLOCKS = 8
RNN_BLOCK = RNN_WIDTH // RNN_BLOCKS
LRU_C = 8.0
CONV_W = 4

ML_HEADS = 4
ML_DIM = 256
ML_WIDTH = ML_HEADS * ML_DIM
ML_GATES = 2 * 2 * ML_HEADS
ML_CHUNK = 128

D_FF = ((8 * D_MODEL + 3 * 256 - 1) // (3 * 256)) * 256

IN_SPLITS = (DA_WIDTH, DA_WIDTH, DA_WIDTH, RNN_WIDTH, RNN_WIDTH,
             ML_WIDTH, ML_WIDTH, ML_WIDTH, ML_WIDTH, ML_GATES, N_BRANCH * D_MODEL)
N_IN = sum(IN_SPLITS)

kernel_name = 'hybrid_diffattn_rglru_mlstm_prefix_dit'


def rms_norm(x, g):
    xf = x.astype(jnp.float32)
    y = xf * lax.rsqrt(jnp.mean(jnp.square(xf), axis=-1, keepdims=True) + EPS)
    return (y * g.astype(jnp.float32)).astype(x.dtype)


def modulate(x, g, shift, scale):
    return rms_norm(x, g) * (1 + scale) + shift


def split_inputs(u):
    cuts = []
    acc = 0
    for w in IN_SPLITS[:-1]:
        acc += w
        cuts.append(acc)
    return jnp.split(u, cuts, axis=-1)


def dwconv(x, w, b):
    ch = x.shape[-1]
    lo = (CONV_W - 1) // 2
    hi = CONV_W - 1 - lo
    y = lax.conv_general_dilated(x, w.astype(x.dtype)[:, None, :], window_strides=(1,),
                                 padding=[(lo, hi)], dimension_numbers=('NWC', 'WIO', 'NWC'),
                                 feature_group_count=ch)
    return y + b.astype(x.dtype)


def axial_rope_tables(n):
    rows = n // GRID_W
    row = jnp.repeat(jnp.arange(rows, dtype=jnp.float32), GRID_W)
    col = jnp.tile(jnp.arange(GRID_W, dtype=jnp.float32), rows)
    quarter = DA_DIM // 4
    inv = ROPE_BASE ** (-jnp.arange(quarter, dtype=jnp.float32) / quarter)
    ar = row[:, None] * inv
    ac = col[:, None] * inv
    ang = jnp.concatenate([ar, ar, ac, ac], axis=-1)
    return jnp.cos(ang), jnp.sin(ang)


def apply_axial_rope(x, cos, sin):
    cs = cos[:, None, None, :].astype(x.dtype)
    sn = sin[:, None, None, :].astype(x.dtype)
    p0, p1, p2, p3 = jnp.split(x, 4, axis=-1)
    return x * cs + jnp.concatenate([-p1, p0, -p3, p2], axis=-1) * sn


def da_heads(u, g):
    return rms_norm(u.reshape(*u.shape[:-1], DA_HEADS, 2, DA_DIM), g)


def to_heads(t):
    return t.transpose(0, 2, 1, 3, 4)


def da_values(u):
    b, n, _ = u.shape
    return u.reshape(b, n, DA_HEADS, 2 * DA_DIM).transpose(0, 2, 1, 3)


def diff_attention_scores(q, k, v, lam):
    s = jnp.einsum('bhqmd,bhkmd->bhmqk', q, k).astype(jnp.float32) * DA_SCALE
    p = jax.nn.softmax(s, axis=-1)
    w = p[:, :, 0] - lam * p[:, :, 1]
    return jnp.einsum('bhqk,bhkv->bhqv', w.astype(v.dtype), v)


def latent_diff_attention(q, k, v, lam):
    b, h, n = q.shape[:3]
    nb = n // Q_BLOCK
    qb = jnp.moveaxis(q.reshape(b, h, nb, Q_BLOCK, 2, DA_DIM), 2, 0)
    o = lax.map(lambda qq: diff_attention_scores(qq, k, v, lam), qb)
    return jnp.moveaxis(o, 0, 2).reshape(b, h, n, 2 * DA_DIM)


def da_output(o, g, lam_init):
    b, h, n, _ = o.shape
    o = rms_norm(o, g) * (1.0 - lam_init)
    return o.transpose(0, 2, 1, 3).reshape(b, n, DA_WIDTH)


def linear_scan(a, b, h0, reverse):
    if reverse:
        a = jnp.flip(a, axis=1)
        b = jnp.flip(b, axis=1)

    def combine(e1, e2):
        return e1[0] * e2[0], e2[0] * e1[1] + e2[1]

    a_cum, b_cum = lax.associative_scan(combine, (a, b), axis=1)
    h = a_cum * h0[:, None, :] + b_cum
    return jnp.flip(h, axis=1) if reverse else h


def rglru_coeffs(x, wa, ba, wx, bx, lam):
    xf = x.astype(jnp.float32)
    xb = xf.reshape(*xf.shape[:-1], RNN_BLOCKS, RNN_BLOCK)
    r = jax.nn.sigmoid(jnp.einsum('bnkc,kcd->bnkd', xb, wa.astype(jnp.float32)).reshape(xf.shape)
                       + ba.astype(jnp.float32))
    i = jax.nn.sigmoid(jnp.einsum('bnkc,kcd->bnkd', xb, wx.astype(jnp.float32)).reshape(xf.shape)
                       + bx.astype(jnp.float32))
    log_a = -LRU_C * r * jax.nn.softplus(-lam.astype(jnp.float32))
    return jnp.exp(log_a), jnp.sqrt(-jnp.expm1(2.0 * log_a)) * (i * xf)


def rglru_mixer(x_c, x_l, gate_c, gate_l, conv_w, conv_b, wa, ba, wx, bx, lam, need_ctx):
    xc = dwconv(x_c, conv_w, conv_b)
    xl = dwconv(x_l, conv_w, conv_b)
    bsz = xc.shape[0]
    hs_l, hs_c = [], []
    for d, rev in ((0, False), (1, True)):
        ac, bc = rglru_coeffs(xc, wa[d], ba[d], wx[d], bx[d], lam[d])
        hc = linear_scan(ac, bc, jnp.zeros((bsz, RNN_WIDTH), jnp.float32), rev)
        h0 = hc[:, 0] if rev else hc[:, -1]
        al, bl = rglru_coeffs(xl, wa[d], ba[d], wx[d], bx[d], lam[d])
        hs_l.append(linear_scan(al, bl, h0, rev))
        hs_c.append(hc)
    y_l = ((hs_l[0] + hs_l[1]) * jax.nn.gelu(gate_l.astype(jnp.float32))).astype(x_l.dtype)
    y_c = None
    if need_ctx:
        y_c = ((hs_c[0] + hs_c[1]) * jax.nn.gelu(gate_c.astype(jnp.float32))).astype(x_c.dtype)
    return y_l, y_c


def mlstm_scan(q, k, v, ig, lf, state):
    b, h, n, _ = q.shape
    nc = n // ML_CHUNK

    def chunks(t):
        return jnp.moveaxis(t.reshape(b, h, nc, ML_CHUNK, *t.shape[3:]), 2, 0)

    tril = jnp.tril(jnp.ones((ML_CHUNK, ML_CHUNK), dtype=bool))

    def step(carry, xs):
        cmat, nvec, m = carry
        qc, kc, vc, ic, fc = xs
        bcum = jnp.cumsum(fc, axis=-1)
        g_inter = bcum + m[..., None]
        dmat = jnp.where(tril, bcum[..., :, None] - bcum[..., None, :] + ic[..., None, :], -jnp.inf)
        m_t = jnp.maximum(g_inter, jnp.max(dmat, axis=-1))
        w_inter = jnp.exp(g_inter - m_t)
        s = jnp.einsum('bhtd,bhsd->bhts', qc, kc) * jnp.exp(dmat - m_t[..., None])
        num = jnp.einsum('bhts,bhsv->bhtv', s, vc) + w_inter[..., None] * jnp.einsum('bhvd,bhtd->bhtv', cmat, qc)
        den = jnp.sum(s, axis=-1) + w_inter * jnp.einsum('bhd,bhtd->bht', nvec, qc)
        hout = num / jnp.maximum(jnp.abs(den), jnp.exp(-m_t))[..., None]
        decay = bcum[..., -1] + m
        w_s = bcum[..., -1:] - bcum + ic
        m_new = jnp.maximum(decay, jnp.max(w_s, axis=-1))
        ws = jnp.exp(w_s - m_new[..., None])
        sc = jnp.exp(decay - m_new)
        c_new = sc[..., None, None] * cmat + jnp.einsum('bhsv,bhsd->bhvd', ws[..., None] * vc, kc)
        n_new = sc[..., None] * nvec + jnp.einsum('bhs,bhsd->bhd', ws, kc)
        return (c_new, n_new, m_new), hout

    state, hseq = lax.scan(step, state, (chunks(q), chunks(k), chunks(v), chunks(ig), chunks(lf)))
    return jnp.moveaxis(hseq, 0, 2).reshape(b, h, n, -1), state


def mlstm_prep(q, k, v, gp, conv_w, conv_b, gate_b):
    b, n, _ = q.shape
    qk = jax.nn.silu(dwconv(jnp.concatenate([q, k], axis=-1), conv_w, conv_b))
    q, k = jnp.split(qk, 2, axis=-1)

    def heads(t):
        return t.reshape(b, n, ML_HEADS, ML_DIM).transpose(0, 2, 1, 3).astype(jnp.float32)

    g = (gp.astype(jnp.float32) + gate_b.astype(jnp.float32)).reshape(b, n, 2, 2, ML_HEADS)
    g = g.transpose(2, 3, 0, 4, 1)
    return heads(q), heads(k) * (ML_DIM ** -0.5), heads(v), g[:, 0], jax.nn.log_sigmoid(g[:, 1])


def mlstm_readout(hsum, o, norm_g):
    b, h, n, d = hsum.shape
    hn = rms_norm(hsum, norm_g.reshape(ML_HEADS, 1, ML_DIM))
    hn = hn.transpose(0, 2, 1, 3).reshape(b, n, ML_WIDTH)
    return (hn * jax.nn.sigmoid(o.astype(jnp.float32))).astype(o.dtype)


def mlstm_mixer(q_c, k_c, v_c, o_c, gp_c, q_l, k_l, v_l, o_l, gp_l, conv_w, conv_b, gate_b, norm_g, need_ctx):
    qc, kc, vc, ic, fc = mlstm_prep(q_c, k_c, v_c, gp_c, conv_w, conv_b, gate_b)
    ql, kl, vl, il, fl = mlstm_prep(q_l, k_l, v_l, gp_l, conv_w, conv_b, gate_b)
    bsz = qc.shape[0]
    init = (jnp.zeros((bsz, ML_HEADS, ML_DIM, ML_DIM), jnp.float32),
            jnp.zeros((bsz, ML_HEADS, ML_DIM), jnp.float32),
            jnp.zeros((bsz, ML_HEADS), jnp.float32))
    hs_l, hs_c = [], []
    for d, rev in ((0, False), (1, True)):
        def flip(t):
            return jnp.flip(t, axis=2) if rev else t
        hc, st = mlstm_scan(flip(qc), flip(kc), flip(vc), flip(ic[d]), flip(fc[d]), init)
        hl, _ = mlstm_scan(flip(ql), flip(kl), flip(vl), flip(il[d]), flip(fl[d]), st)
        hs_l.append(flip(hl))
        hs_c.append(flip(hc))
    y_l = mlstm_readout(hs_l[0] + hs_l[1], o_l, norm_g)
    y_c = mlstm_readout(hs_c[0] + hs_c[1], o_c, norm_g) if need_ctx else None
    return y_l, y_c


def merge_branches(gates, a, r, m, wba, wbr, wbm, wo):
    ga, gr, gm = jnp.split(gates, N_BRANCH, axis=-1)
    z = jax.nn.sigmoid(ga) * (a @ wba) + jax.nn.sigmoid(gr) * (r @ wbr) + jax.nn.sigmoid(gm) * (m @ wbm)
    return z @ wo


def swiglu_ffn(h, w1, w3, w2):
    return (jax.nn.silu(h @ w1) * (h @ w3)) @ w2


def setup_inputs(seed: int = 0) -> dict:
    key = jax.random.key(seed)
    ks = jax.random.split(key, 40)
    f32 = jnp.float32

    def nrm(k, shape, scale):
        return jax.random.normal(k, shape, f32) * scale

    def gain(k, shape):
        return 1.0 + 0.05 * jax.random.normal(k, shape, f32)

    a0 = jax.random.uniform(ks[19], (DEPTH, 2, RNN_WIDTH), f32, minval=0.9, maxval=0.999)
    p = a0 ** (1.0 / LRU_C)
    rnn_lambda = jnp.log(p) - jnp.log1p(-p)
    ib = 0.1 * jax.random.normal(ks[22], (DEPTH, 2, 1, ML_HEADS), f32)
    fb = jnp.linspace(3.0, 6.0, ML_HEADS, dtype=f32) + 0.1 * jax.random.normal(ks[32], (DEPTH, 2, 1, ML_HEADS), f32)
    ml_gate_b = jnp.concatenate([ib, fb], axis=2).reshape(DEPTH, ML_GATES)
    return {
        'x': nrm(ks[0], (BATCH, SEQ, D_MODEL), 1.0),
        'c': nrm(ks[1], (BATCH, D_MODEL), 1.0),
        'ctx': nrm(ks[2], (BATCH, CTX_LEN, D_MODEL), 1.0),
        'c_ctx': nrm(ks[3], (D_MODEL,), 1.0),
        'w_mod': nrm(ks[4], (DEPTH, D_MODEL, N_MOD * D_MODEL), 0.5 * D_MODEL ** -0.5),
        'b_mod': nrm(ks[5], (DEPTH, N_MOD * D_MODEL), 0.02),
        'norm1_g': gain(ks[6], (DEPTH, D_MODEL)),
        'norm2_g': gain(ks[7], (DEPTH, D_MODEL)),
        'w_in': nrm(ks[8], (DEPTH, D_MODEL, N_IN), D_MODEL ** -0.5),
        'attn_qnorm_g': gain(ks[9], (DEPTH, DA_DIM)),
        'attn_knorm_g': gain(ks[10], (DEPTH, DA_DIM)),
        'attn_lambda': nrm(ks[11], (DEPTH, 4, DA_DIM), 0.1),
        'attn_subln_g': gain(ks[12], (DEPTH, 2 * DA_DIM)),
        'rnn_conv_w': nrm(ks[13], (DEPTH, CONV_W, RNN_WIDTH), CONV_W ** -0.5),
        'rnn_conv_b': nrm(ks[14], (DEPTH, RNN_WIDTH), 0.02),
        'rnn_wa': nrm(ks[15], (DEPTH, 2, RNN_BLOCKS, RNN_BLOCK, RNN_BLOCK), RNN_BLOCK ** -0.5),
        'rnn_ba': nrm(ks[16], (DEPTH, 2, RNN_WIDTH), 0.02),
        'rnn_wx': nrm(ks[17], (DEPTH, 2, RNN_BLOCKS, RNN_BLOCK, RNN_BLOCK), RNN_BLOCK ** -0.5),
        'rnn_bx': nrm(ks[18], (DEPTH, 2, RNN_WIDTH), 0.02),
        'rnn_lambda': rnn_lambda,
        'ml_conv_w': nrm(ks[20], (DEPTH, CONV_W, 2 * ML_WIDTH), CONV_W ** -0.5),
        'ml_conv_b': nrm(ks[21], (DEPTH, 2 * ML_WIDTH), 0.02),
        'ml_gate_b': ml_gate_b,
        'ml_norm_g': gain(ks[23], (DEPTH, ML_WIDTH)),
        'w_branch_attn': nrm(ks[24], (DEPTH, DA_WIDTH, D_MODEL), DA_WIDTH ** -0.5),
        'w_branch_rnn': nrm(ks[25], (DEPTH, RNN_WIDTH, D_MODEL), RNN_WIDTH ** -0.5),
        'w_branch_ml': nrm(ks[26], (DEPTH, ML_WIDTH, D_MODEL), ML_WIDTH ** -0.5),
        'w_out': nrm(ks[27], (DEPTH, D_MODEL, D_MODEL), D_MODEL ** -0.5),
        'w_ffn1': nrm(ks[28], (DEPTH, D_MODEL, D_FF), D_MODEL ** -0.5),
        'w_ffn3': nrm(ks[29], (DEPTH, D_MODEL, D_FF), D_MODEL ** -0.5),
        'w_ffn2': nrm(ks[30], (DEPTH, D_FF, D_MODEL), D_FF ** -0.5),
    }


def reference(x, c, ctx, c_ctx, w_mod, b_mod, norm1_g, norm2_g, w_in, attn_qnorm_g, attn_knorm_g,
              attn_lambda, attn_subln_g, rnn_conv_w, rnn_conv_b, rnn_wa, rnn_ba, rnn_wx, rnn_bx,
              rnn_lambda, ml_conv_w, ml_conv_b, ml_gate_b, ml_norm_g, w_branch_attn, w_branch_rnn,
              w_branch_ml, w_out, w_ffn1, w_ffn3, w_ffn2):
    n = x.shape[1]
    cos, sin = axial_rope_tables(n)
    s_lat = jax.nn.silu(c)
    s_ctx = jax.nn.silu(c_ctx)[None, :]
    h, hc = x, ctx
    for l in range(DEPTH):
        need_ctx = l < DEPTH - 1
        lam_init = 0.8 - 0.6 * math.exp(-0.3 * l)
        sh1, sc1, gt1, sh2, sc2, gt2 = jnp.split((s_lat @ w_mod[l] + b_mod[l])[:, None, :], N_MOD, axis=-1)
        csh1, csc1, cgt1, csh2, csc2, cgt2 = jnp.split((s_ctx @ w_mod[l] + b_mod[l])[:, None, :], N_MOD, axis=-1)

        (aq_l, ak_l, av_l, rx_l, rg_l, mq_l, mk_l, mv_l, mo_l, mg_l, bg_l) = split_inputs(
            modulate(h, norm1_g[l], sh1, sc1) @ w_in[l])
        (aq_c, ak_c, av_c, rx_c, rg_c, mq_c, mk_c, mv_c, mo_c, mg_c, bg_c) = split_inputs(
            modulate(hc, norm1_g[l], csh1, csc1) @ w_in[l])

        lv = attn_lambda[l].astype(jnp.float32)
        lam = jnp.exp(jnp.sum(lv[0] * lv[1])) - jnp.exp(jnp.sum(lv[2] * lv[3])) + lam_init
        q_lat = to_heads(apply_axial_rope(da_heads(aq_l, attn_qnorm_g[l]), cos, sin))
        k_lat = to_heads(apply_axial_rope(da_heads(ak_l, attn_knorm_g[l]), cos, sin))
        k_ctx = to_heads(da_heads(ak_c, attn_knorm_g[l]))
        v_ctx = da_values(av_c)
        k_all = jnp.concatenate([k_lat, k_ctx], axis=2)
        v_all = jnp.concatenate([da_values(av_l), v_ctx], axis=2)
        a_l = da_output(latent_diff_attention(q_lat, k_all, v_all, lam), attn_subln_g[l], lam_init)

        r_l, r_c = rglru_mixer(rx_c, rx_l, rg_c, rg_l, rnn_conv_w[l], rnn_conv_b[l], rnn_wa[l], rnn_ba[l],
                               rnn_wx[l], rnn_bx[l], rnn_lambda[l], need_ctx)

        m_l, m_c = mlstm_mixer(mq_c, mk_c, mv_c, mo_c, mg_c, mq_l, mk_l, mv_l, mo_l, mg_l,
                               ml_conv_w[l], ml_conv_b[l], ml_gate_b[l], ml_norm_g[l], need_ctx)

        h = h + gt1 * merge_branches(bg_l, a_l, r_l, m_l, w_branch_attn[l], w_branch_rnn[l], w_branch_ml[l], w_out[l])
        h = h + gt2 * swiglu_ffn(modulate(h, norm2_g[l], sh2, sc2), w_ffn1[l], w_ffn3[l], w_ffn2[l])

        if need_ctx:
            q_ctx = to_heads(da_heads(aq_c, attn_qnorm_g[l]))
            a_c = da_output(diff_attention_scores(q_ctx, k_ctx, v_ctx, lam), attn_subln_g[l], lam_init)
            hc = hc + cgt1 * merge_branches(bg_c, a_c, r_c, m_c, w_branch_attn[l], w_branch_rnn[l], w_branch_ml[l], w_out[l])
            hc = hc + cgt2 * swiglu_ffn(modulate(hc, norm2_g[l], csh2, csc2), w_ffn1[l], w_ffn3[l], w_ffn2[l])
    return h
```

```python
import functools
import math

import jax
import jax.numpy as jnp
from jax import lax
from jax.experimental import pallas as pl
from jax.experimental.pallas import tpu as pltpu

F32 = jnp.float32
BF16 = jnp.bfloat16

EPS = 1e-6
GRID_W = 64
ROPE_BASE = 10000.0
N_MOD = 6
N_BRANCH = 3
MOD_ROWS = 8

DA_HEADS = 4
DA_DIM = 128
DA_WIDTH = DA_HEADS * 2 * DA_DIM
DA_SCALE = DA_DIM ** -0.5

RNN_WIDTH = 1024
RNN_BLOCKS = 8
RNN_BLOCK = RNN_WIDTH // RNN_BLOCKS
LRU_C = 8.0
CONV_W = 4

ML_HEADS = 4
ML_DIM = 256
ML_WIDTH = ML_HEADS * ML_DIM
ML_GATES = 2 * 2 * ML_HEADS
ML_CHUNK = 128

LANE = 128
SUBLANE = 8
VMEM_LIMIT = 56 * 1024 * 1024

OFF_AQ, OFF_AK, OFF_AV = 0, DA_WIDTH, 2 * DA_WIDTH
OFF_MQ = 3 * DA_WIDTH
OFF_MK, OFF_MV, OFF_MO = OFF_MQ + ML_WIDTH, OFF_MQ + 2 * ML_WIDTH, OFF_MQ + 3 * ML_WIDTH
OFF_BG = OFF_MQ + 4 * ML_WIDTH
OFF_RX, OFF_RG, OFF_MG = 0, RNN_WIDTH, 2 * RNN_WIDTH
NPF = 2 * RNN_WIDTH + 2 * LANE

TN_PROJ = 512
ROW_CHUNK = 16


def _cparams(sem):
    return pltpu.CompilerParams(dimension_semantics=sem, vmem_limit_bytes=VMEM_LIMIT)


def _modvec_kernel(s_ref, w_ref, b_ref, o_ref):
    s = s_ref[...]
    s = s * jax.nn.sigmoid(s)
    o_ref[...] = jnp.dot(s.astype(BF16), w_ref[...].astype(BF16), preferred_element_type=F32) + b_ref[...]


def _modvec(cvecs, w_mod, b_mod):
    depth, d, nm = w_mod.shape
    rows = cvecs.shape[0]
    tn = 1024
    return pl.pallas_call(
        _modvec_kernel,
        grid=(depth, nm // tn),
        in_specs=[
            pl.BlockSpec((rows, d), lambda l, j: (0, 0)),
            pl.BlockSpec((None, d, tn), lambda l, j: (l, 0, j)),
            pl.BlockSpec((None, 1, tn), lambda l, j: (l, 0, j)),
        ],
        out_specs=pl.BlockSpec((None, rows, tn), lambda l, j: (l, 0, j)),
        out_shape=jax.ShapeDtypeStruct((depth, rows, nm), F32),
        compiler_params=_cparams(("arbitrary", "arbitrary")),
        name="modvec",
    )(cvecs, w_mod, b_mod.reshape(depth, 1, nm))


def _norm_modulate(h_ref, ml_ref, mc_ref, g_ref, xn_ref, *, row0, seq, shift_row):
    tm = xn_ref.shape[0]
    g = g_ref[...]
    sh_l, sc_l = ml_ref[shift_row:shift_row + 1, :], ml_ref[shift_row + 1:shift_row + 2, :]
    sh_c, sc_c = mc_ref[shift_row:shift_row + 1, :], mc_ref[shift_row + 1:shift_row + 2, :]

    def body(c, carry):
        r0 = pl.multiple_of(c * ROW_CHUNK, ROW_CHUNK)
        x = h_ref[pl.ds(r0, ROW_CHUNK), :]
        ms = jnp.mean(x * x, axis=-1, keepdims=True)
        y = x * lax.rsqrt(ms + EPS) * g
        is_ctx = (row0 + r0) >= seq
        sh = jnp.where(is_ctx, sh_c, sh_l)
        sc = jnp.where(is_ctx, sc_c, sc_l)
        xn_ref[pl.ds(r0, ROW_CHUNK), :] = (y * (1.0 + sc) + sh).astype(BF16)
        return carry

    lax.fori_loop(0, tm // ROW_CHUNK, body, 0)


def _gated_residual(h_ref, acc_ref, ml_ref, mc_ref, o_ref, *, row0, seq, gate_row):
    tm = acc_ref.shape[0]
    gt_l = ml_ref[gate_row:gate_row + 1, :]
    gt_c = mc_ref[gate_row:gate_row + 1, :]

    def body(c, carry):
        r0 = pl.multiple_of(c * ROW_CHUNK, ROW_CHUNK)
        gt = jnp.where((row0 + r0) >= seq, gt_c, gt_l)
        o_ref[pl.ds(r0, ROW_CHUNK), :] = h_ref[pl.ds(r0, ROW_CHUNK), :] + gt * acc_ref[pl.ds(r0, ROW_CHUNK), :]
        return carry

    lax.fori_loop(0, tm // ROW_CHUNK, body, 0)


def _inproj_kernel(h_ref, ml_ref, mc_ref, g_ref, w_ref, qkg_ref, cos_ref, sa_ref, sb_ref, u_ref, xn_ref, acc_ref,
                   *, seq, n_q_tiles, n_k_tiles):
    i, j = pl.program_id(1), pl.program_id(2)
    tm, tn = acc_ref.shape

    @pl.when(j == 0)
    def _():
        _norm_modulate(h_ref, ml_ref, mc_ref, g_ref, xn_ref, row0=i * tm, seq=seq, shift_row=0)

    def qk_epilogue(gain, scale):
        rc = 32

        def body(c, carry):
            r0 = pl.multiple_of(c * rc, rc)
            cs, sa, sb = cos_ref[pl.ds(r0, rc), :], sa_ref[pl.ds(r0, rc), :], sb_ref[pl.ds(r0, rc), :]
            for gidx in range(tn // DA_DIM):
                lanes = slice(gidx * DA_DIM, (gidx + 1) * DA_DIM)
                x = acc_ref[pl.ds(r0, rc), lanes]
                ms = jnp.mean(x * x, axis=-1, keepdims=True)
                y = x * lax.rsqrt(ms + EPS) * gain
                y = y * cs + pltpu.roll(y, DA_DIM - DA_DIM // 4, 1) * sa + pltpu.roll(y, DA_DIM // 4, 1) * sb
                u_ref[pl.ds(r0, rc), lanes] = (y * scale).astype(u_ref.dtype)
            return carry

        lax.fori_loop(0, tm // rc, body, 0)

    if n_q_tiles + n_k_tiles > 0:
        @pl.when(j < n_q_tiles)
        def _():
            acc_ref[...] = jnp.dot(xn_ref[...], w_ref[...], preferred_element_type=F32)
            qk_epilogue(qkg_ref[0:1, :], DA_SCALE)

        @pl.when(jnp.logical_and(j >= n_q_tiles, j < n_q_tiles + n_k_tiles))
        def _():
            acc_ref[...] = jnp.dot(xn_ref[...], w_ref[...], preferred_element_type=F32)
            qk_epilogue(qkg_ref[1:2, :], 1.0)

        @pl.when(j >= n_q_tiles + n_k_tiles)
        def _():
            u_ref[...] = jnp.dot(xn_ref[...], w_ref[...], preferred_element_type=F32).astype(u_ref.dtype)
    else:
        u_ref[...] = jnp.dot(xn_ref[...], w_ref[...], preferred_element_type=F32).astype(u_ref.dtype)


def _inproj(h, mod, norm_g, w, l, qkg, rope, *, seq, tm, tn, out_dtype, qk):
    bsz, t, d = h.shape
    npad = w.shape[-1]
    nj = npad // tn
    n_q = (DA_WIDTH // tn) if qk else 0
    cos, sa, sb = rope
    kern = functools.partial(_inproj_kernel, seq=seq, n_q_tiles=n_q, n_k_tiles=n_q)
    return pl.pallas_call(
        kern,
        grid=(bsz, t // tm, nj),
        in_specs=[
            pl.BlockSpec((None, tm, d), lambda b, i, j: (b, i, 0)),
            pl.BlockSpec((None, MOD_ROWS, d), lambda b, i, j: (b, 0, 0)),
            pl.BlockSpec((None, MOD_ROWS, d), lambda b, i, j: (bsz, 0, 0)),
            pl.BlockSpec((None, 1, d), lambda b, i, j: (l, 0, 0)),
            pl.BlockSpec((None, d, tn), lambda b, i, j: (l, 0, j)),
            pl.BlockSpec((None, 2, DA_DIM), lambda b, i, j: (l, 0, 0)),
            pl.BlockSpec((tm, DA_DIM), lambda b, i, j: (i, 0)),
            pl.BlockSpec((tm, DA_DIM), lambda b, i, j: (i, 0)),
            pl.BlockSpec((tm, DA_DIM), lambda b, i, j: (i, 0)),
        ],
        out_specs=pl.BlockSpec((tm, tn), lambda b, i, j: (i, b * nj + j)),
        out_shape=jax.ShapeDtypeStruct((t, bsz * npad), out_dtype),
        scratch_shapes=[pltpu.VMEM((tm, d), BF16), pltpu.VMEM((tm, tn), F32)],
        compiler_params=_cparams(("arbitrary", "arbitrary", "arbitrary")),
        name="inproj_bf16" if qk else "inproj_f32",
    )(h, mod, mod, norm_g, w, qkg, cos, sa, sb)


def _attn_kernel(lam_ref, q_ref, k_ref, v_ref, g_ref, o_ref, *, seq, lam_init, n_lat_tiles):
    qi = pl.program_id(2)
    lv = lam_ref[...]
    lam = (jnp.exp(jnp.sum(lv[0:1] * lv[1:2], axis=-1, keepdims=True))
           - jnp.exp(jnp.sum(lv[2:3] * lv[3:4], axis=-1, keepdims=True)) + lam_init)
    nt = (((1,), (1,)), ((), ()))

    def attend(k, v):
        q = q_ref[...]
        s0 = lax.dot_general(q[:, :DA_DIM], k[:, :DA_DIM], nt, preferred_element_type=F32)
        s1 = lax.dot_general(q[:, DA_DIM:], k[:, DA_DIM:], nt, preferred_element_type=F32)
        p0 = jnp.exp(s0 - jnp.max(s0, axis=-1, keepdims=True))
        p1 = jnp.exp(s1 - jnp.max(s1, axis=-1, keepdims=True))
        w = p0 * (1.0 / jnp.sum(p0, axis=-1, keepdims=True)) - p1 * (lam / jnp.sum(p1, axis=-1, keepdims=True))
        o = jnp.dot(w.astype(BF16), v, preferred_element_type=F32)
        ms = jnp.mean(o * o, axis=-1, keepdims=True)
        o_ref[...] = (o * lax.rsqrt(ms + EPS) * g_ref[...] * (1.0 - lam_init)).astype(o_ref.dtype)

    @pl.when(qi < n_lat_tiles)
    def _():
        attend(k_ref[...], v_ref[...])

    @pl.when(qi >= n_lat_tiles)
    def _():
        attend(k_ref[seq:, :], v_ref[seq:, :])


def _attention(ub, attn_lambda, subln_g, l, *, bsz, seq, npb, tq, lam_init):
    t = ub.shape[0]
    hw = 2 * DA_DIM
    cpb = npb // hw
    kern = functools.partial(_attn_kernel, seq=seq, lam_init=lam_init, n_lat_tiles=seq // tq)
    return pl.pallas_call(
        kern,
        grid=(bsz, DA_HEADS, t // tq),
        in_specs=[
            pl.BlockSpec((None, 4, DA_DIM), lambda b, h, qi: (l, 0, 0)),
            pl.BlockSpec((tq, hw), lambda b, h, qi: (qi, b * cpb + OFF_AQ // hw + h)),
            pl.BlockSpec((t, hw), lambda b, h, qi: (0, b * cpb + OFF_AK // hw + h)),
            pl.BlockSpec((t, hw), lambda b, h, qi: (0, b * cpb + OFF_AV // hw + h)),
            pl.BlockSpec((None, 1, hw), lambda b, h, qi: (l, 0, 0)),
        ],
        out_specs=pl.BlockSpec((tq, hw), lambda b, h, qi: (qi, b * DA_HEADS + h)),
        out_shape=jax.ShapeDtypeStruct((t, bsz * DA_WIDTH), BF16),
        compiler_params=_cparams(("arbitrary", "arbitrary", "arbitrary")),
        name="diff_attention",
    )(attn_lambda, ub, ub, ub, subln_g)


def _merge_kernel(h_ref, ml_ref, mc_ref, a_ref, r_ref, m_ref, ga_ref, gr_ref, gm_ref, wa_ref, wr_ref, wm_ref, wo_ref,
                  o_ref, acc_ref, *, seq):
    i, j = pl.program_id(1), pl.program_id(2)
    tm = acc_ref.shape[0]

    def branch(x_ref, g_ref, w_ref):
        y = jnp.dot(x_ref[...].astype(BF16), w_ref[...], preferred_element_type=F32)
        return jax.nn.sigmoid(g_ref[...].astype(F32)) * y

    z = branch(a_ref, ga_ref, wa_ref) + branch(r_ref, gr_ref, wr_ref) + branch(m_ref, gm_ref, wm_ref)
    part = jnp.dot(z.astype(BF16), wo_ref[...], preferred_element_type=F32)

    @pl.when(j == 0)
    def _():
        acc_ref[...] = part

    @pl.when(j > 0)
    def _():
        acc_ref[...] += part

    @pl.when(j == pl.num_programs(2) - 1)
    def _():
        _gated_residual(h_ref, acc_ref, ml_ref, mc_ref, o_ref, row0=i * tm, seq=seq, gate_row=2)


def _merge(h, mod, a, r, m, ub, wba, wbr, wbm, wo, l, *, seq, npb, tm, tz):
    bsz, t, d = h.shape
    nz = d // tz
    gpb = npb // tz
    g0 = OFF_BG // tz

    def gate_spec(k):
        return pl.BlockSpec((tm, tz), lambda b, i, j: (i, b * gpb + g0 + k * nz + j))

    def branch_spec(width):
        return pl.BlockSpec((tm, width), lambda b, i, j: (i, b))

    def w_spec(width):
        return pl.BlockSpec((None, width, tz), lambda b, i, j: (l, 0, j))

    return pl.pallas_call(
        functools.partial(_merge_kernel, seq=seq),
        grid=(bsz, t // tm, nz),
        in_specs=[
            pl.BlockSpec((None, tm, d), lambda b, i, j: (b, i, 0)),
            pl.BlockSpec((None, MOD_ROWS, d), lambda b, i, j: (b, 0, 0)),
            pl.BlockSpec((None, MOD_ROWS, d), lambda b, i, j: (bsz, 0, 0)),
            branch_spec(DA_WIDTH), branch_spec(RNN_WIDTH), branch_spec(ML_WIDTH),
            gate_spec(0), gate_spec(1), gate_spec(2),
            w_spec(DA_WIDTH), w_spec(RNN_WIDTH), w_spec(ML_WIDTH),
            pl.BlockSpec((None, tz, d), lambda b, i, j: (l, j, 0)),
        ],
        out_specs=pl.BlockSpec((None, tm, d), lambda b, i, j: (b, i, 0)),
        out_shape=jax.ShapeDtypeStruct(h.shape, F32),
        scratch_shapes=[pltpu.VMEM((tm, d), F32)],
        compiler_params=_cparams(("arbitrary", "arbitrary", "arbitrary")),
        name="merge_out",
    )(h, mod, mod, a, r, m, ub, ub, ub, wba, wbr, wbm, wo)


def _ffn_kernel(h_ref, ml_ref, mc_ref, g_ref, w1_ref, w3_ref, w2_ref, o_ref, xn_ref, acc_ref, *, seq):
    i, j = pl.program_id(1), pl.program_id(2)
    tm = acc_ref.shape[0]

    @pl.when(j == 0)
    def _():
        _norm_modulate(h_ref, ml_ref, mc_ref, g_ref, xn_ref, row0=i * tm, seq=seq, shift_row=3)

    xn = xn_ref[...]
    u = jnp.dot(xn, w1_ref[...], preferred_element_type=F32)
    v = jnp.dot(xn, w3_ref[...], preferred_element_type=F32)
    part = jnp.dot((u * jax.nn.sigmoid(u) * v).astype(BF16), w2_ref[...], preferred_element_type=F32)

    @pl.when(j == 0)
    def _():
        acc_ref[...] = part

    @pl.when(j > 0)
    def _():
        acc_ref[...] += part

    @pl.when(j == pl.num_programs(2) - 1)
    def _():
        _gated_residual(h_ref, acc_ref, ml_ref, mc_ref, o_ref, row0=i * tm, seq=seq, gate_row=5)


def _ffn(h, mod, norm_g, w1, w3, w2, l, *, seq, tm, tf):
    bsz, t, d = h.shape
    dff = w1.shape[-1]
    return pl.pallas_call(
        functools.partial(_ffn_kernel, seq=seq),
        grid=(bsz, t // tm, dff // tf),
        in_specs=[
            pl.BlockSpec((None, tm, d), lambda b, i, j: (b, i, 0)),
            pl.BlockSpec((None, MOD_ROWS, d), lambda b, i, j: (b, 0, 0)),
            pl.BlockSpec((None, MOD_ROWS, d), lambda b, i, j: (bsz, 0, 0)),
            pl.BlockSpec((None, 1, d), lambda b, i, j: (l, 0, 0)),
            pl.BlockSpec((None, d, tf), lambda b, i, j: (l, 0, j)),
            pl.BlockSpec((None, d, tf), lambda b, i, j: (l, 0, j)),
            pl.BlockSpec((None, tf, d), lambda b, i, j: (l, j, 0)),
        ],
        out_specs=pl.BlockSpec((None, tm, d), lambda b, i, j: (b, i, 0)),
        out_shape=jax.ShapeDtypeStruct(h.shape, F32),
        scratch_shapes=[pltpu.VMEM((tm, d), BF16), pltpu.VMEM((tm, d), F32)],
        compiler_params=_cparams(("arbitrary", "arbitrary", "arbitrary")),
        name="swiglu_ffn",
    )(h, mod, mod, norm_g, w1, w3, w2)


def _scan_tile(s, n_lat, n_ctx, rev):
    if rev:
        return jnp.where(s < n_ctx, n_lat + n_ctx - 1 - s, n_lat - 1 - (s - n_ctx))
    return jnp.where(s < n_ctx, n_lat + s, s - n_ctx)


def _halo_valid(tile, n_lat, n_ctx):
    prev_ok = jnp.logical_and(tile != 0, tile != n_lat)
    next_ok = jnp.logical_and(tile != n_lat - 1, tile != n_lat + n_ctx - 1)
    return prev_ok.astype(F32), next_ok.astype(F32)


def _softplus(x):
    return jnp.maximum(x, 0.0) + jnp.log1p(jnp.exp(-jnp.abs(x)))


RG_CB = 512
RG_TS = 128
RG_TC = 32


def _rglru_kernel(*refs, rev, n_lat, n_ctx):
    if rev:
        x_ref, xp_ref, xn_ref, cw_ref, cb_ref, w_ref, bias_ref, g_ref, hf_ref, o_ref, xw_s, a_s, b_s, h_s = refs
    else:
        x_ref, xp_ref, xn_ref, cw_ref, cb_ref, w_ref, bias_ref, o_ref, xw_s, a_s, b_s, h_s = refs
    s = pl.program_id(1)
    ts = x_ref.shape[0]
    tile = _scan_tile(s, n_lat, n_ctx, rev)
    prev_ok, next_ok = _halo_valid(tile, n_lat, n_ctx)

    @pl.when(s == 0)
    def _():
        h_s[...] = jnp.zeros_like(h_s)

    xw_s[0:1] = xp_ref[...] * prev_ok
    xw_s[1:ts + 1] = x_ref[...]
    xw_s[ts + 1:ts + 3] = xn_ref[...] * next_ok

    ba, bx = bias_ref[0:1, :], bias_ref[1:2, :]
    sp = _softplus(-bias_ref[2:3, :])
    cw = [cw_ref[k:k + 1, :] for k in range(CONV_W)]
    cb = cb_ref[...]

    def gates(c, carry):
        t0 = pl.multiple_of(c * RG_TC, RG_TC)
        conv = cb + sum(cw[k] * xw_s[pl.ds(t0 + k, RG_TC)] for k in range(CONV_W))
        x2 = conv.reshape(RG_TC * SUBLANE, RG_CB)
        for blk in range(RG_CB // RNN_BLOCK):
            lanes = slice(blk * RNN_BLOCK, (blk + 1) * RNN_BLOCK)
            xb = x2[:, lanes]
            z = jnp.dot(xb.astype(BF16), w_ref[blk], preferred_element_type=F32)
            r = jax.nn.sigmoid(z[:, :RNN_BLOCK] + ba[:, lanes])
            ig = jax.nn.sigmoid(z[:, RNN_BLOCK:] + bx[:, lanes])
            log_a = -LRU_C * r * sp[:, lanes]
            a = jnp.exp(log_a)
            bb = jnp.sqrt(1.0 - a * a) * (ig * xb)
            a_s[pl.ds(t0, RG_TC), :, lanes] = a.reshape(RG_TC, SUBLANE, RNN_BLOCK)
            b_s[pl.ds(t0, RG_TC), :, lanes] = bb.reshape(RG_TC, SUBLANE, RNN_BLOCK)
        return carry

    lax.fori_loop(0, ts // RG_TC, gates, 0)

    def step(k, h):
        tt = (ts - 1 - k) if rev else k
        h = a_s[tt] * h + b_s[tt]
        if rev:
            o_ref[tt] = (h + hf_ref[tt]) * jax.nn.gelu(g_ref[tt])
        else:
            o_ref[tt] = h
        return h

    h_s[...] = lax.fori_loop(0, ts, step, h_s[...], unroll=8)


def _rglru(uf3, conv_w, conv_b, w_gate, bias, l, d, hf3, *, n_lat, n_ctx):
    t, bsz, npf = uf3.shape
    ts = RG_TS
    rev = d == 1
    tile = functools.partial(_scan_tile, n_lat=n_lat, n_ctx=n_ctx, rev=rev)
    nblk = RG_CB // RNN_BLOCK
    in_specs = [
        pl.BlockSpec((ts, bsz, RG_CB), lambda kc, s: (tile(s), 0, OFF_RX // RG_CB + kc)),
        pl.BlockSpec((1, bsz, RG_CB), lambda kc, s: (jnp.maximum(tile(s) * ts - 1, 0), 0, OFF_RX // RG_CB + kc)),
        pl.BlockSpec((2, bsz, RG_CB),
                     lambda kc, s: (jnp.minimum((tile(s) + 1) * (ts // 2), t // 2 - 1), 0, OFF_RX // RG_CB + kc)),
        pl.BlockSpec((None, CONV_W, RG_CB), lambda kc, s: (l, 0, kc)),
        pl.BlockSpec((None, 1, RG_CB), lambda kc, s: (l, 0, kc)),
        pl.BlockSpec((None, None, nblk, RNN_BLOCK, 2 * RNN_BLOCK), lambda kc, s: (l, d, kc, 0, 0)),
        pl.BlockSpec((None, None, SUBLANE, RG_CB), lambda kc, s: (l, d, 0, kc)),
    ]
    args = [uf3, uf3, uf3, conv_w, conv_b, w_gate, bias]
    if rev:
        in_specs += [
            pl.BlockSpec((ts, bsz, RG_CB), lambda kc, s: (tile(s), 0, OFF_RG // RG_CB + kc)),
            pl.BlockSpec((ts, bsz, RG_CB), lambda kc, s: (tile(s), 0, kc)),
        ]
        args += [uf3, hf3]
    return pl.pallas_call(
        functools.partial(_rglru_kernel, rev=rev, n_lat=n_lat, n_ctx=n_ctx),
        grid=(RNN_WIDTH // RG_CB, n_lat + n_ctx),
        in_specs=in_specs,
        out_specs=pl.BlockSpec((ts, bsz, RG_CB), lambda kc, s: (tile(s), 0, kc)),
        out_shape=jax.ShapeDtypeStruct((t, bsz, RNN_WIDTH), F32),
        scratch_shapes=[
            pltpu.VMEM((ts + CONV_W - 1, bsz, RG_CB), F32),
            pltpu.VMEM((ts, bsz, RG_CB), F32),
            pltpu.VMEM((ts, bsz, RG_CB), F32),
            pltpu.VMEM((bsz, RG_CB), F32),
        ],
        compiler_params=_cparams(("arbitrary", "arbitrary")),
        name="rglru_rev" if rev else "rglru_fwd",
    )(*args)


ML_HALO = 16


def _mlstm_kernel(*refs, rev, n_lat, n_ctx):
    if rev:
        (q_ref, qp_ref, qn_ref, k_ref, kp_ref, kn_ref, v_ref, gt_ref, cw_ref, cb_ref, gb_ref,
         og_ref, ng_ref, hf_ref, o_ref, c_s, n_s, m_s) = refs
    else:
        (q_ref, qp_ref, qn_ref, k_ref, kp_ref, kn_ref, v_ref, gt_ref, cw_ref, cb_ref, gb_ref,
         o_ref, c_s, n_s, m_s) = refs
    s = pl.program_id(1)
    L = ML_CHUNK
    tile = _scan_tile(s, n_lat, n_ctx, rev)
    prev_ok, next_ok = _halo_valid(tile, n_lat, n_ctx)

    @pl.when(s == 0)
    def _():
        c_s[...] = jnp.zeros_like(c_s)
        n_s[...] = jnp.zeros_like(n_s)
        m_s[...] = jnp.zeros_like(m_s)

    row = lax.broadcasted_iota(jnp.int32, (L, 1), 0)

    def conv_silu(x_ref, xp_ref, xn_ref, lanes):
        x = x_ref[...].astype(F32)
        xp = xp_ref[ML_HALO - 1:ML_HALO, :].astype(F32) * prev_ok
        xn0 = xn_ref[0:1, :].astype(F32) * next_ok
        xn1 = xn_ref[1:2, :].astype(F32) * next_ok
        x_m1 = jnp.where(row == 0, xp, pltpu.roll(x, 1, 0))
        x_p1 = jnp.where(row == L - 1, xn0, pltpu.roll(x, L - 1, 0))
        x_p2 = jnp.where(row == L - 1, xn1, jnp.where(row == L - 2, xn0, pltpu.roll(x, L - 2, 0)))
        y = (cb_ref[:, lanes] + cw_ref[0:1, lanes] * x_m1 + cw_ref[1:2, lanes] * x
             + cw_ref[2:3, lanes] * x_p1 + cw_ref[3:4, lanes] * x_p2)
        return y * jax.nn.sigmoid(y)

    qc = conv_silu(q_ref, qp_ref, qn_ref, slice(0, ML_WIDTH)).astype(BF16)
    kc = (conv_silu(k_ref, kp_ref, kn_ref, slice(ML_WIDTH, 2 * ML_WIDTH)) * (ML_DIM ** -0.5)).astype(BF16)
    gates = gt_ref[...] + gb_ref[...]

    ti = lax.broadcasted_iota(jnp.int32, (L, L), 0)
    si = lax.broadcasted_iota(jnp.int32, (L, L), 1)
    causal = (si >= ti) if rev else (si <= ti)
    nt = (((1,), (1,)), ((), ()))
    tn = (((0,), (0,)), ((), ()))
    dsel = 2 * ML_HEADS if rev else 0

    for h in range(ML_HEADS):
        lanes = slice(h * ML_DIM, (h + 1) * ML_DIM)
        q, k, v = qc[:, lanes], kc[:, lanes], v_ref[:, lanes]
        ig = gates[:, dsel + h:dsel + h + 1]
        fg = gates[:, dsel + ML_HEADS + h:dsel + ML_HEADS + h + 1]
        lf = jnp.minimum(fg, 0.0) - jnp.log1p(jnp.exp(-jnp.abs(fg)))
        m_prev = m_s[h][0:1, 0:1]

        lf_row = jnp.transpose(jnp.broadcast_to(lf, (L, L)))
        bcum = jnp.sum(jnp.where(causal, lf_row, 0.0), axis=-1, keepdims=True)
        e_row = jnp.transpose(jnp.broadcast_to(bcum - ig, (L, L)))
        dmat = jnp.where(causal, bcum - e_row, -jnp.inf)
        g_inter = bcum + m_prev
        m_t = jnp.maximum(g_inter, jnp.max(dmat, axis=-1, keepdims=True))
        w_inter = jnp.exp(g_inter - m_t)
        smat = lax.dot_general(q, k, nt, preferred_element_type=F32) * jnp.exp(dmat - m_t)
        cmat = c_s[h]
        nvec = n_s[h]
        num = (jnp.dot(smat.astype(BF16), v, preferred_element_type=F32)
               + w_inter * lax.dot_general(q, cmat.astype(BF16), nt, preferred_element_type=F32))
        den = (jnp.sum(smat, axis=-1, keepdims=True)
               + w_inter * jnp.sum(q.astype(F32) * nvec, axis=-1, keepdims=True))
        hout = num / jnp.maximum(jnp.abs(den), jnp.exp(-m_t))

        total = jnp.sum(lf, axis=0, keepdims=True)
        decay = total + m_prev
        w_s = total - bcum + ig
        m_new = jnp.maximum(decay, jnp.max(w_s, axis=0, keepdims=True))
        ws = jnp.exp(w_s - m_new)
        sc = jnp.exp(decay - m_new)
        wv = (ws * v.astype(F32)).astype(BF16)
        c_s[h] = sc * cmat + lax.dot_general(wv, k, tn, preferred_element_type=F32)
        n_s[h] = sc * nvec + jnp.sum(ws * k.astype(F32), axis=0, keepdims=True)
        m_s[h] = jnp.broadcast_to(m_new, m_s.shape[1:])

        if rev:
            hsum = hout + hf_ref[:, lanes]
            ms = jnp.mean(hsum * hsum, axis=-1, keepdims=True)
            hn = hsum * lax.rsqrt(ms + EPS) * ng_ref[:, lanes]
            o_ref[:, lanes] = (hn * jax.nn.sigmoid(og_ref[:, lanes].astype(F32))).astype(o_ref.dtype)
        else:
            o_ref[:, lanes] = hout


def _mlstm(ub, uf, conv_w, conv_b, gate_b, norm_g, l, d, hf, *, bsz, npb, n_lat, n_ctx):
    t = ub.shape[0]
    L = ML_CHUNK
    rev = d == 1
    tile = functools.partial(_scan_tile, n_lat=n_lat, n_ctx=n_ctx, rev=rev)
    wpb = npb // ML_WIDTH
    hpb = L // ML_HALO

    def main(off):
        return pl.BlockSpec((L, ML_WIDTH), lambda b, s: (tile(s), b * wpb + off // ML_WIDTH))

    def prev(off):
        return pl.BlockSpec((ML_HALO, ML_WIDTH),
                            lambda b, s: (jnp.maximum(tile(s) * hpb - 1, 0), b * wpb + off // ML_WIDTH))

    def nxt(off):
        return pl.BlockSpec((ML_HALO, ML_WIDTH),
                            lambda b, s: (jnp.minimum((tile(s) + 1) * hpb, t // ML_HALO - 1), b * wpb + off // ML_WIDTH))

    gw = 2 * LANE
    in_specs = [
        main(OFF_MQ), prev(OFF_MQ), nxt(OFF_MQ), main(OFF_MK), prev(OFF_MK), nxt(OFF_MK), main(OFF_MV),
        pl.BlockSpec((L, gw), lambda b, s: (tile(s), b * (NPF // gw) + OFF_MG // gw)),
        pl.BlockSpec((None, CONV_W, 2 * ML_WIDTH), lambda b, s: (l, 0, 0)),
        pl.BlockSpec((None, 1, 2 * ML_WIDTH), lambda b, s: (l, 0, 0)),
        pl.BlockSpec((None, 1, gw), lambda b, s: (l, 0, 0)),
    ]
    args = [ub, ub, ub, ub, ub, ub, ub, uf, conv_w, conv_b, gate_b]
    if rev:
        in_specs += [
            main(OFF_MO),
            pl.BlockSpec((None, 1, ML_WIDTH), lambda b, s: (l, 0, 0)),
            pl.BlockSpec((L, ML_WIDTH), lambda b, s: (tile(s), b)),
        ]
        args += [ub, norm_g, hf]
    return pl.pallas_call(
        functools.partial(_mlstm_kernel, rev=rev, n_lat=n_lat, n_ctx=n_ctx),
        grid=(bsz, n_lat + n_ctx),
        in_specs=in_specs,
        out_specs=pl.BlockSpec((L, ML_WIDTH), lambda b, s: (tile(s), b)),
        out_shape=jax.ShapeDtypeStruct((t, bsz * ML_WIDTH), BF16 if rev else F32),
        scratch_shapes=[
            pltpu.VMEM((ML_HEADS, ML_DIM, ML_DIM), F32),
            pltpu.VMEM((ML_HEADS, 1, ML_DIM), F32),
            pltpu.VMEM((ML_HEADS, SUBLANE, LANE), F32),
        ],
        compiler_params=_cparams(("arbitrary", "arbitrary")),
        name="mlstm_rev" if rev else "mlstm_fwd",
    )(*args)


def _rope_tables(seq, ctx_len):
    rows = seq // GRID_W
    row = jnp.repeat(jnp.arange(rows, dtype=F32), GRID_W)
    col = jnp.tile(jnp.arange(GRID_W, dtype=F32), rows)
    quarter = DA_DIM // 4
    inv = ROPE_BASE ** (-jnp.arange(quarter, dtype=F32) / quarter)
    ar, ac = row[:, None] * inv, col[:, None] * inv
    ang = jnp.concatenate([ar, ar, ac, ac], axis=-1)
    cos, sin = jnp.cos(ang), jnp.sin(ang)
    lane = jnp.arange(DA_DIM) // quarter
    sin_a = jnp.where(lane % 2 == 0, -sin, 0.0)
    sin_b = jnp.where(lane % 2 == 1, sin, 0.0)
    pad = lambda tbl, fill: jnp.concatenate([tbl, jnp.full((ctx_len, DA_DIM), fill, F32)], axis=0)
    return pad(cos, 1.0), pad(sin_a, 0.0), pad(sin_b, 0.0)


def _row_tile(t, parts, seq):
    tm = t // parts
    assert tm * parts == t and tm % ROW_CHUNK == 0 and seq % ROW_CHUNK == 0, (t, parts)
    return tm


def kernel(x, c, ctx, c_ctx, w_mod, b_mod, norm1_g, norm2_g, w_in, attn_qnorm_g, attn_knorm_g, attn_lambda,
           attn_subln_g, rnn_conv_w, rnn_conv_b, rnn_wa, rnn_ba, rnn_wx, rnn_bx, rnn_lambda, ml_conv_w, ml_conv_b,
           ml_gate_b, ml_norm_g, w_branch_attn, w_branch_rnn, w_branch_ml, w_out, w_ffn1, w_ffn3, w_ffn2):
    bsz, seq, d = x.shape
    ctx_len = ctx.shape[1]
    depth = w_in.shape[0]
    t = seq + ctx_len
    assert bsz == SUBLANE and seq % 256 == 0 and ctx_len % 256 == 0 and d % TN_PROJ == 0

    cuts = [0]
    for w in (DA_WIDTH, DA_WIDTH, DA_WIDTH, RNN_WIDTH, RNN_WIDTH, ML_WIDTH, ML_WIDTH, ML_WIDTH, ML_WIDTH, ML_GATES,
              N_BRANCH * d):
        cuts.append(cuts[-1] + w)
    seg = lambda k: w_in[:, :, cuts[k]:cuts[k + 1]]
    w_b = jnp.concatenate([seg(0), seg(1), seg(2), seg(5), seg(6), seg(7), seg(8), seg(10)], axis=-1).astype(BF16)
    npb = w_b.shape[-1]
    w_f = jnp.concatenate([seg(3), seg(4), seg(9), jnp.zeros((depth, d, 2 * LANE - ML_GATES), F32)],
                          axis=-1).astype(BF16)
    assert npb % ML_WIDTH == 0 and w_f.shape[-1] == NPF
    qkg = jnp.stack([attn_qnorm_g, attn_knorm_g], axis=1)
    rope = _rope_tables(seq, ctx_len)
    w_gate = jnp.concatenate([rnn_wa, rnn_wx], axis=-1).astype(BF16)
    rg_bias = jnp.concatenate([rnn_ba[:, :, None], rnn_bx[:, :, None], rnn_lambda[:, :, None],
                               jnp.zeros((depth, 2, SUBLANE - 3, RNN_WIDTH), F32)], axis=2)
    gate_b = jnp.pad(ml_gate_b, ((0, 0), (0, 2 * LANE - ML_GATES)))[:, None, :]
    wba, wbr, wbm, wo = (w.astype(BF16) for w in (w_branch_attn, w_branch_rnn, w_branch_ml, w_out))
    w1, w3, w2 = (w.astype(BF16) for w in (w_ffn1, w_ffn3, w_ffn2))
    row3 = lambda a: a[:, None, :]

    cvecs = jnp.concatenate([c, c_ctx[None, :], jnp.zeros((2 * SUBLANE - bsz - 1, d), F32)], axis=0)
    mod_all = _modvec(cvecs, w_mod, b_mod)[:, :bsz + 1]
    mod_all = jnp.pad(mod_all.reshape(depth, bsz + 1, N_MOD, d), ((0, 0), (0, 0), (0, MOD_ROWS - N_MOD), (0, 0)))

    h = jnp.concatenate([x, ctx], axis=1)
    tm_proj = _row_tile(t, 4, seq)
    tm_half = _row_tile(t, 8, seq)
    n_lat_rg, n_ctx_rg = seq // RG_TS, ctx_len // RG_TS
    n_lat_ml, n_ctx_ml = seq // ML_CHUNK, ctx_len // ML_CHUNK

    for l in range(depth):
        lam_init = 0.8 - 0.6 * math.exp(-0.3 * l)
        mod = mod_all[l]
        ub = _inproj(h, mod, row3(norm1_g), w_b, l, qkg, rope, seq=seq, tm=tm_proj, tn=TN_PROJ, out_dtype=BF16,
                     qk=True)
        uf = _inproj(h, mod, row3(norm1_g), w_f, l, qkg, rope, seq=seq, tm=tm_proj, tn=NPF // 3, out_dtype=F32,
                     qk=False)
        a = _attention(ub, attn_lambda, row3(attn_subln_g), l, bsz=bsz, seq=seq, npb=npb, tq=256, lam_init=lam_init)
        uf3 = uf.reshape(t, bsz, NPF)
        hf3 = _rglru(uf3, rnn_conv_w, row3(rnn_conv_b), w_gate, rg_bias, l, 0, None, n_lat=n_lat_rg, n_ctx=n_ctx_rg)
        r = _rglru(uf3, rnn_conv_w, row3(rnn_conv_b), w_gate, rg_bias, l, 1, hf3, n_lat=n_lat_rg, n_ctx=n_ctx_rg)
        r = r.reshape(t, bsz * RNN_WIDTH)
        mf = _mlstm(ub, uf, ml_conv_w, row3(ml_conv_b), gate_b, row3(ml_norm_g), l, 0, None, bsz=bsz, npb=npb,
                    n_lat=n_lat_ml, n_ctx=n_ctx_ml)
        m = _mlstm(ub, uf, ml_conv_w, row3(ml_conv_b), gate_b, row3(ml_norm_g), l, 1, mf, bsz=bsz, npb=npb,
                   n_lat=n_lat_ml, n_ctx=n_ctx_ml)
        h = _merge(h, mod, a, r, m, ub, wba, wbr, wbm, wo, l, seq=seq, npb=npb, tm=tm_half, tz=TN_PROJ)
        h = _ffn(h, mod, row3(norm2_g), w1, w3, w2, l, seq=seq, tm=tm_half, tf=TN_PROJ)
    return h[:, :seq, :]
```

```python
import functools
import math

import jax
import jax.numpy as jnp
from jax import lax
from jax.experimental import pallas as pl
from jax.experimental.pallas import tpu as pltpu

F32 = jnp.float32
BF16 = jnp.bfloat16

EPS = 1e-6
GRID_W = 64
ROPE_BASE = 10000.0
N_MOD = 6
N_BRANCH = 3
MOD_ROWS = 8

DA_HEADS = 4
DA_DIM = 128
DA_WIDTH = DA_HEADS * 2 * DA_DIM
DA_SCALE = DA_DIM ** -0.5

RNN_WIDTH = 1024
RNN_BLOCKS = 8
RNN_BLOCK = RNN_WIDTH // RNN_BLOCKS
LRU_C = 8.0
CONV_W = 4

ML_HEADS = 4
ML_DIM = 256
ML_WIDTH = ML_HEADS * ML_DIM
ML_GATES = 2 * 2 * ML_HEADS
ML_CHUNK = 128

LANE = 128
SUBLANE = 8
VMEM_LIMIT = 56 * 1024 * 1024

OFF_AQ, OFF_AK, OFF_AV = 0, DA_WIDTH, 2 * DA_WIDTH
OFF_MQ = 3 * DA_WIDTH
OFF_MK, OFF_MV, OFF_MO = OFF_MQ + ML_WIDTH, OFF_MQ + 2 * ML_WIDTH, OFF_MQ + 3 * ML_WIDTH
OFF_RX = OFF_MQ + 4 * ML_WIDTH
OFF_RG = OFF_RX + RNN_WIDTH
OFF_BG = OFF_RG + RNN_WIDTH
GATE_PAD = LANE

TN_PROJ = 512
ROW_CHUNK = 16


def _cparams(sem):
    return pltpu.CompilerParams(dimension_semantics=sem, vmem_limit_bytes=VMEM_LIMIT)


def _modvec_kernel(s_ref, w_ref, b_ref, o_ref):
    s = s_ref[...]
    s = s * jax.nn.sigmoid(s)
    o_ref[...] = jnp.dot(s.astype(BF16), w_ref[...].astype(BF16), preferred_element_type=F32) + b_ref[...]


def _modvec(cvecs, w_mod, b_mod):
    depth, d, nm = w_mod.shape
    rows = cvecs.shape[0]
    tn = 1024
    return pl.pallas_call(
        _modvec_kernel,
        grid=(depth, nm // tn),
        in_specs=[
            pl.BlockSpec((rows, d), lambda l, j: (0, 0)),
            pl.BlockSpec((None, d, tn), lambda l, j: (l, 0, j)),
            pl.BlockSpec((None, 1, tn), lambda l, j: (l, 0, j)),
        ],
        out_specs=pl.BlockSpec((None, rows, tn), lambda l, j: (l, 0, j)),
        out_shape=jax.ShapeDtypeStruct((depth, rows, nm), F32),
        compiler_params=_cparams(("arbitrary", "arbitrary")),
        name="modvec",
    )(cvecs, w_mod, b_mod.reshape(depth, 1, nm))


def _norm_modulate(h_ref, ml_ref, mc_ref, g_ref, xn_ref, *, row0, seq, shift_row):
    tm = xn_ref.shape[0]
    g = g_ref[...]
    sh_l, sc_l = ml_ref[shift_row:shift_row + 1, :], ml_ref[shift_row + 1:shift_row + 2, :]
    sh_c, sc_c = mc_ref[shift_row:shift_row + 1, :], mc_ref[shift_row + 1:shift_row + 2, :]

    def body(c, carry):
        r0 = pl.multiple_of(c * ROW_CHUNK, ROW_CHUNK)
        x = h_ref[pl.ds(r0, ROW_CHUNK), :]
        ms = jnp.mean(x * x, axis=-1, keepdims=True)
        y = x * lax.rsqrt(ms + EPS) * g
        is_ctx = (row0 + r0) >= seq
        sh = jnp.where(is_ctx, sh_c, sh_l)
        sc = jnp.where(is_ctx, sc_c, sc_l)
        xn_ref[pl.ds(r0, ROW_CHUNK), :] = (y * (1.0 + sc) + sh).astype(BF16)
        return carry

    lax.fori_loop(0, tm // ROW_CHUNK, body, 0, unroll=2)


def _gated_residual(h_ref, acc_ref, ml_ref, mc_ref, o_ref, *, row0, seq, gate_row):
    tm = acc_ref.shape[0]
    gt_l = ml_ref[gate_row:gate_row + 1, :]
    gt_c = mc_ref[gate_row:gate_row + 1, :]

    def body(c, carry):
        r0 = pl.multiple_of(c * ROW_CHUNK, ROW_CHUNK)
        gt = jnp.where((row0 + r0) >= seq, gt_c, gt_l)
        o_ref[pl.ds(r0, ROW_CHUNK), :] = h_ref[pl.ds(r0, ROW_CHUNK), :] + gt * acc_ref[pl.ds(r0, ROW_CHUNK), :]
        return carry

    lax.fori_loop(0, tm // ROW_CHUNK, body, 0, unroll=2)


EPILOGUE_ROWS_CAP = 272


def _epilogue_rows(tm):
    return max(r for r in range(ROW_CHUNK, min(tm, EPILOGUE_ROWS_CAP) + 1, ROW_CHUNK) if tm % r == 0)


def _inproj_kernel(h_ref, ml_ref, mc_ref, g_ref, w_ref, wg_ref, qkg_ref, cos_ref, sa_ref, sb_ref, u_ref, gt_ref,
                   xn_ref, acc_ref, *, seq, n_q_tiles, n_k_tiles):
    i, j = pl.program_id(1), pl.program_id(2)
    tm, tn = acc_ref.shape

    @pl.when(j == 0)
    def _():
        _norm_modulate(h_ref, ml_ref, mc_ref, g_ref, xn_ref, row0=i * tm, seq=seq, shift_row=0)
        gt_ref[...] = jnp.dot(xn_ref[...], wg_ref[...], preferred_element_type=F32)

    def qk_epilogue(gain, scale):
        rc = _epilogue_rows(tm)

        def body(c, carry):
            r0 = pl.multiple_of(c * rc, rc)
            cs, sa, sb = cos_ref[pl.ds(r0, rc), :], sa_ref[pl.ds(r0, rc), :], sb_ref[pl.ds(r0, rc), :]
            for gidx in range(tn // DA_DIM):
                lanes = slice(gidx * DA_DIM, (gidx + 1) * DA_DIM)
                x = acc_ref[pl.ds(r0, rc), lanes]
                ms = jnp.mean(x * x, axis=-1, keepdims=True)
                y = x * lax.rsqrt(ms + EPS) * gain
                y = y * cs + pltpu.roll(y, DA_DIM - DA_DIM // 4, 1) * sa + pltpu.roll(y, DA_DIM // 4, 1) * sb
                u_ref[pl.ds(r0, rc), lanes] = (y * scale).astype(u_ref.dtype)
            return carry

        lax.fori_loop(0, tm // rc, body, 0)

    @pl.when(j < n_q_tiles)
    def _():
        acc_ref[...] = jnp.dot(xn_ref[...], w_ref[...], preferred_element_type=F32)
        qk_epilogue(qkg_ref[0:1, :], DA_SCALE * math.log2(math.e))

    @pl.when(jnp.logical_and(j >= n_q_tiles, j < n_q_tiles + n_k_tiles))
    def _():
        acc_ref[...] = jnp.dot(xn_ref[...], w_ref[...], preferred_element_type=F32)
        qk_epilogue(qkg_ref[1:2, :], 1.0)

    @pl.when(j >= n_q_tiles + n_k_tiles)
    def _():
        u_ref[...] = jnp.dot(xn_ref[...], w_ref[...], preferred_element_type=F32).astype(u_ref.dtype)


def _inproj(h, mod, norm_g, w, w_gate, l, qkg, rope, *, seq, tm, tn):
    bsz, t, d = h.shape
    npad = w.shape[-1]
    nj = npad // tn
    n_q = DA_WIDTH // tn
    cos, sa, sb = rope
    kern = functools.partial(_inproj_kernel, seq=seq, n_q_tiles=n_q, n_k_tiles=n_q)
    return pl.pallas_call(
        kern,
        grid=(bsz, t // tm, nj),
        in_specs=[
            pl.BlockSpec((None, tm, d), lambda b, i, j: (b, i, 0)),
            pl.BlockSpec((None, MOD_ROWS, d), lambda b, i, j: (b, 0, 0)),
            pl.BlockSpec((None, MOD_ROWS, d), lambda b, i, j: (bsz, 0, 0)),
            pl.BlockSpec((None, 1, d), lambda b, i, j: (l, 0, 0)),
            pl.BlockSpec((None, d, tn), lambda b, i, j: (l, 0, j)),
            pl.BlockSpec((None, d, GATE_PAD), lambda b, i, j: (l, 0, 0)),
            pl.BlockSpec((None, 2, DA_DIM), lambda b, i, j: (l, 0, 0)),
            pl.BlockSpec((tm, DA_DIM), lambda b, i, j: (i, 0)),
            pl.BlockSpec((tm, DA_DIM), lambda b, i, j: (i, 0)),
            pl.BlockSpec((tm, DA_DIM), lambda b, i, j: (i, 0)),
        ],
        out_specs=[
            pl.BlockSpec((tm, tn), lambda b, i, j: (i, b * nj + j)),
            pl.BlockSpec((tm, GATE_PAD), lambda b, i, j: (i, b)),
        ],
        out_shape=[
            jax.ShapeDtypeStruct((t, bsz * npad), BF16),
            jax.ShapeDtypeStruct((t, bsz * GATE_PAD), F32),
        ],
        scratch_shapes=[pltpu.VMEM((tm, d), BF16), pltpu.VMEM((tm, tn), F32)],
        compiler_params=_cparams(("arbitrary", "arbitrary", "arbitrary")),
        name="inproj",
    )(h, mod, mod, norm_g, w, w_gate, qkg, cos, sa, sb)


ATT_KB = 256


def _attn_kernel(lam_ref, q_ref, k_ref, v_ref, g_ref, o_ref, s_s, p_s, *, seq, lam_init, n_lat_tiles):
    qi = pl.program_id(2)
    tq = q_ref.shape[0]
    n_chunks = k_ref.shape[0] // ATT_KB
    lv = lam_ref[...]
    lam = (jnp.exp(jnp.sum(lv[0:1] * lv[1:2], axis=-1, keepdims=True))
           - jnp.exp(jnp.sum(lv[2:3] * lv[3:4], axis=-1, keepdims=True)) + lam_init)
    nt = (((1,), (1,)), ((), ()))

    def lane_groups(x):
        return [x[:, g * LANE:(g + 1) * LANE] for g in range(x.shape[1] // LANE)]

    def scores(mp, c_lo, c_hi):
        lanes = slice(mp * DA_DIM, (mp + 1) * DA_DIM)
        q = q_ref[:, lanes]
        m_part = jnp.full((tq, LANE), -jnp.inf, F32)
        for c in range(c_lo, c_hi):
            keys = slice(c * ATT_KB, (c + 1) * ATT_KB)
            s = lax.dot_general(q, k_ref[keys, lanes], nt, preferred_element_type=F32)
            s_s[mp, :, keys] = s
            for sg in lane_groups(s):
                m_part = jnp.maximum(m_part, sg)
        return jnp.max(m_part, axis=-1, keepdims=True)

    def exponentials(mp, m, c_lo, c_hi):
        l_part = jnp.zeros((tq, LANE), F32)
        for c in range(c_lo, c_hi):
            keys = slice(c * ATT_KB, (c + 1) * ATT_KB)
            p = jnp.exp2(s_s[mp, :, keys] - m)
            for pg in lane_groups(p):
                l_part = l_part + pg
            p_s[mp, :, keys] = p.astype(BF16)
        return jnp.sum(l_part, axis=-1, keepdims=True)

    def attend(c_lo, c_hi):
        span = slice(c_lo * ATT_KB, c_hi * ATT_KB)
        m0, m1 = scores(0, c_lo, c_hi), scores(1, c_lo, c_hi)
        l0, l1 = exponentials(0, m0, c_lo, c_hi), exponentials(1, m1, c_lo, c_hi)
        o0 = jnp.dot(p_s[0, :, span], v_ref[span, :], preferred_element_type=F32)
        o1 = jnp.dot(p_s[1, :, span], v_ref[span, :], preferred_element_type=F32)
        o = o0 * (1.0 / l0) - o1 * (lam / l1)
        ms = jnp.mean(o * o, axis=-1, keepdims=True)
        o_ref[...] = (o * lax.rsqrt(ms + EPS) * g_ref[...] * (1.0 - lam_init)).astype(o_ref.dtype)

    @pl.when(qi < n_lat_tiles)
    def _():
        attend(0, n_chunks)

    @pl.when(qi >= n_lat_tiles)
    def _():
        attend(seq // ATT_KB, n_chunks)


def _attention(ub, attn_lambda, subln_g, l, *, bsz, seq, npb, tq, lam_init):
    t = ub.shape[0]
    hw = 2 * DA_DIM
    cpb = npb // hw
    kern = functools.partial(_attn_kernel, seq=seq, lam_init=lam_init, n_lat_tiles=seq // tq)
    return pl.pallas_call(
        kern,
        grid=(bsz, DA_HEADS, t // tq),
        in_specs=[
            pl.BlockSpec((None, 4, DA_DIM), lambda b, h, qi: (l, 0, 0)),
            pl.BlockSpec((tq, hw), lambda b, h, qi: (qi, b * cpb + OFF_AQ // hw + h)),
            pl.BlockSpec((t, hw), lambda b, h, qi: (0, b * cpb + OFF_AK // hw + h)),
            pl.BlockSpec((t, hw), lambda b, h, qi: (0, b * cpb + OFF_AV // hw + h)),
            pl.BlockSpec((None, 1, hw), lambda b, h, qi: (l, 0, 0)),
        ],
        out_specs=pl.BlockSpec((tq, hw), lambda b, h, qi: (qi, b * DA_HEADS + h)),
        out_shape=jax.ShapeDtypeStruct((t, bsz * DA_WIDTH), BF16),
        scratch_shapes=[pltpu.VMEM((2, tq, t), F32), pltpu.VMEM((2, tq, t), BF16)],
        compiler_params=_cparams(("arbitrary", "arbitrary", "arbitrary")),
        name="diff_attention",
    )(attn_lambda, ub, ub, ub, subln_g)


def _merge_kernel(h_ref, ml_ref, mc_ref, a_ref, r_ref, m_ref, ga_ref, gr_ref, gm_ref, wa_ref, wr_ref, wm_ref, wo_ref,
                  o_ref, acc_ref, *, seq):
    i, j = pl.program_id(1), pl.program_id(2)
    tm = acc_ref.shape[0]

    def branch(x_ref, g_ref, w_ref):
        y = jnp.dot(x_ref[...].astype(BF16), w_ref[...], preferred_element_type=F32)
        return jax.nn.sigmoid(g_ref[...].astype(F32)) * y

    z = branch(a_ref, ga_ref, wa_ref) + branch(r_ref, gr_ref, wr_ref) + branch(m_ref, gm_ref, wm_ref)
    part = jnp.dot(z.astype(BF16), wo_ref[...], preferred_element_type=F32)

    @pl.when(j == 0)
    def _():
        acc_ref[...] = part

    @pl.when(j > 0)
    def _():
        acc_ref[...] += part

    @pl.when(j == pl.num_programs(2) - 1)
    def _():
        _gated_residual(h_ref, acc_ref, ml_ref, mc_ref, o_ref, row0=i * tm, seq=seq, gate_row=2)


def _merge(h, mod, a, r, m, ub, wba, wbr, wbm, wo, l, *, seq, npb, tm, tz):
    bsz, t, d = h.shape
    nz = d // tz
    gpb = npb // tz
    g0 = OFF_BG // tz

    def gate_spec(k):
        return pl.BlockSpec((tm, tz), lambda b, i, j: (i, b * gpb + g0 + k * nz + j))

    def branch_spec(width):
        return pl.BlockSpec((tm, width), lambda b, i, j: (i, b))

    def w_spec(width):
        return pl.BlockSpec((None, width, tz), lambda b, i, j: (l, 0, j))

    return pl.pallas_call(
        functools.partial(_merge_kernel, seq=seq),
        grid=(bsz, t // tm, nz),
        in_specs=[
            pl.BlockSpec((None, tm, d), lambda b, i, j: (b, i, 0)),
            pl.BlockSpec((None, MOD_ROWS, d), lambda b, i, j: (b, 0, 0)),
            pl.BlockSpec((None, MOD_ROWS, d), lambda b, i, j: (bsz, 0, 0)),
            branch_spec(DA_WIDTH), branch_spec(RNN_WIDTH), branch_spec(ML_WIDTH),
            gate_spec(0), gate_spec(1), gate_spec(2),
            w_spec(DA_WIDTH), w_spec(RNN_WIDTH), w_spec(ML_WIDTH),
            pl.BlockSpec((None, tz, d), lambda b, i, j: (l, j, 0)),
        ],
        out_specs=pl.BlockSpec((None, tm, d), lambda b, i, j: (b, i, 0)),
        out_shape=jax.ShapeDtypeStruct(h.shape, F32),
        scratch_shapes=[pltpu.VMEM((tm, d), F32)],
        compiler_params=_cparams(("arbitrary", "arbitrary", "arbitrary")),
        name="merge_out",
    )(h, mod, mod, a, r, m, ub, ub, ub, wba, wbr, wbm, wo)


def _ffn_kernel(h_ref, ml_ref, mc_ref, g_ref, w1_ref, w3_ref, w2_ref, o_ref, xn_ref, acc_ref, *, seq):
    i, j = pl.program_id(1), pl.program_id(2)
    tm = acc_ref.shape[0]

    @pl.when(j == 0)
    def _():
        _norm_modulate(h_ref, ml_ref, mc_ref, g_ref, xn_ref, row0=i * tm, seq=seq, shift_row=3)

    xn = xn_ref[...]
    u = jnp.dot(xn, w1_ref[...], preferred_element_type=F32)
    v = jnp.dot(xn, w3_ref[...], preferred_element_type=F32)
    part = jnp.dot((u * jax.nn.sigmoid(u) * v).astype(BF16), w2_ref[...], preferred_element_type=F32)

    @pl.when(j == 0)
    def _():
        acc_ref[...] = part

    @pl.when(j > 0)
    def _():
        acc_ref[...] += part

    @pl.when(j == pl.num_programs(2) - 1)
    def _():
        _gated_residual(h_ref, acc_ref, ml_ref, mc_ref, o_ref, row0=i * tm, seq=seq, gate_row=5)


def _ffn(h, mod, norm_g, w1, w3, w2, l, *, seq, tm, tf):
    bsz, t, d = h.shape
    dff = w1.shape[-1]
    return pl.pallas_call(
        functools.partial(_ffn_kernel, seq=seq),
        grid=(bsz, t // tm, dff // tf),
        in_specs=[
            pl.BlockSpec((None, tm, d), lambda b, i, j: (b, i, 0)),
            pl.BlockSpec((None, MOD_ROWS, d), lambda b, i, j: (b, 0, 0)),
            pl.BlockSpec((None, MOD_ROWS, d), lambda b, i, j: (bsz, 0, 0)),
            pl.BlockSpec((None, 1, d), lambda b, i, j: (l, 0, 0)),
            pl.BlockSpec((None, d, tf), lambda b, i, j: (l, 0, j)),
            pl.BlockSpec((None, d, tf), lambda b, i, j: (l, 0, j)),
            pl.BlockSpec((None, tf, d), lambda b, i, j: (l, j, 0)),
        ],
        out_specs=pl.BlockSpec((None, tm, d), lambda b, i, j: (b, i, 0)),
        out_shape=jax.ShapeDtypeStruct(h.shape, F32),
        scratch_shapes=[pltpu.VMEM((tm, d), BF16), pltpu.VMEM((tm, d), F32)],
        compiler_params=_cparams(("arbitrary", "arbitrary", "arbitrary")),
        name="swiglu_ffn",
    )(h, mod, mod, norm_g, w1, w3, w2)


def _scan_tile(s, n_lat, n_ctx, rev):
    if rev:
        return jnp.where(s < n_ctx, n_lat + n_ctx - 1 - s, n_lat - 1 - (s - n_ctx))
    return jnp.where(s < n_ctx, n_lat + s, s - n_ctx)


def _halo_valid(tile, n_lat, n_ctx):
    prev_ok = jnp.logical_and(tile != 0, tile != n_lat)
    next_ok = jnp.logical_and(tile != n_lat - 1, tile != n_lat + n_ctx - 1)
    return prev_ok.astype(F32), next_ok.astype(F32)


def _softplus(x):
    return jnp.maximum(x, 0.0) + jnp.log1p(jnp.exp(-jnp.abs(x)))


RG_CB = 512
RG_TS = 128
RG_TC = 32


def _rglru_kernel(*refs, rev, n_lat, n_ctx):
    if rev:
        x_ref, xp_ref, xn_ref, cw_ref, cb_ref, w_ref, bias_ref, g_ref, hf_ref, o_ref, xw_s, a_s, b_s, h_s = refs
    else:
        x_ref, xp_ref, xn_ref, cw_ref, cb_ref, w_ref, bias_ref, o_ref, xw_s, a_s, b_s, h_s = refs
    s = pl.program_id(1)
    ts = x_ref.shape[0]
    tile = _scan_tile(s, n_lat, n_ctx, rev)
    prev_ok, next_ok = _halo_valid(tile, n_lat, n_ctx)

    @pl.when(s == 0)
    def _():
        h_s[...] = jnp.zeros_like(h_s)

    xw_s[0:1] = xp_ref[...] * prev_ok
    xw_s[1:ts + 1] = x_ref[...]
    xw_s[ts + 1:ts + 3] = xn_ref[...] * next_ok

    ba, bx = bias_ref[0:1, :], bias_ref[1:2, :]
    sp = _softplus(-bias_ref[2:3, :])
    cw = [cw_ref[k:k + 1, :] for k in range(CONV_W)]
    cb = cb_ref[...]

    def gates(c, carry):
        t0 = pl.multiple_of(c * RG_TC, RG_TC)
        conv = cb + sum(cw[k] * xw_s[pl.ds(t0 + k, RG_TC)] for k in range(CONV_W))
        x2 = conv.reshape(RG_TC * SUBLANE, RG_CB)
        for blk in range(RG_CB // RNN_BLOCK):
            lanes = slice(blk * RNN_BLOCK, (blk + 1) * RNN_BLOCK)
            xb = x2[:, lanes]
            z = jnp.dot(xb.astype(BF16), w_ref[blk], preferred_element_type=F32)
            r = jax.nn.sigmoid(z[:, :RNN_BLOCK] + ba[:, lanes])
            ig = jax.nn.sigmoid(z[:, RNN_BLOCK:] + bx[:, lanes])
            log_a = -LRU_C * r * sp[:, lanes]
            a = jnp.exp(log_a)
            bb = jnp.sqrt(1.0 - a * a) * (ig * xb)
            a_s[pl.ds(t0, RG_TC), :, lanes] = a.reshape(RG_TC, SUBLANE, RNN_BLOCK)
            b_s[pl.ds(t0, RG_TC), :, lanes] = bb.reshape(RG_TC, SUBLANE, RNN_BLOCK)
        return carry

    lax.fori_loop(0, ts // RG_TC, gates, 0)

    def step(k, h):
        tt = (ts - 1 - k) if rev else k
        h = a_s[tt] * h + b_s[tt]
        if rev:
            o_ref[tt] = (h + hf_ref[tt]) * jax.nn.gelu(g_ref[tt])
        else:
            o_ref[tt] = h
        return h

    h_s[...] = lax.fori_loop(0, ts, step, h_s[...], unroll=8)


def _rglru(uf3, conv_w, conv_b, w_gate, bias, l, d, hf3, *, n_lat, n_ctx):
    t, bsz, _ = uf3.shape
    ts = RG_TS
    rev = d == 1
    tile = functools.partial(_scan_tile, n_lat=n_lat, n_ctx=n_ctx, rev=rev)
    nblk = RG_CB // RNN_BLOCK
    in_specs = [
        pl.BlockSpec((ts, bsz, RG_CB), lambda kc, s: (tile(s), 0, kc)),
        pl.BlockSpec((1, bsz, RG_CB), lambda kc, s: (jnp.maximum(tile(s) * ts - 1, 0), 0, kc)),
        pl.BlockSpec((2, bsz, RG_CB),
                     lambda kc, s: (jnp.minimum((tile(s) + 1) * (ts // 2), t // 2 - 1), 0, kc)),
        pl.BlockSpec((None, CONV_W, RG_CB), lambda kc, s: (l, 0, kc)),
        pl.BlockSpec((None, 1, RG_CB), lambda kc, s: (l, 0, kc)),
        pl.BlockSpec((None, None, nblk, RNN_BLOCK, 2 * RNN_BLOCK), lambda kc, s: (l, d, kc, 0, 0)),
        pl.BlockSpec((None, None, SUBLANE, RG_CB), lambda kc, s: (l, d, 0, kc)),
    ]
    args = [uf3, uf3, uf3, conv_w, conv_b, w_gate, bias]
    if rev:
        in_specs += [
            pl.BlockSpec((ts, bsz, RG_CB), lambda kc, s: (tile(s), 0, RNN_WIDTH // RG_CB + kc)),
            pl.BlockSpec((ts, bsz, RG_CB), lambda kc, s: (tile(s), 0, kc)),
        ]
        args += [uf3, hf3]
    return pl.pallas_call(
        functools.partial(_rglru_kernel, rev=rev, n_lat=n_lat, n_ctx=n_ctx),
        grid=(RNN_WIDTH // RG_CB, n_lat + n_ctx),
        in_specs=in_specs,
        out_specs=pl.BlockSpec((ts, bsz, RG_CB), lambda kc, s: (tile(s), 0, kc)),
        out_shape=jax.ShapeDtypeStruct((t, bsz, RNN_WIDTH), F32),
        scratch_shapes=[
            pltpu.VMEM((ts + CONV_W - 1, bsz, RG_CB), F32),
            pltpu.VMEM((ts, bsz, RG_CB), F32),
            pltpu.VMEM((ts, bsz, RG_CB), F32),
            pltpu.VMEM((bsz, RG_CB), F32),
        ],
        compiler_params=_cparams(("arbitrary", "arbitrary")),
        name="rglru_rev" if rev else "rglru_fwd",
    )(*args)


ML_HALO = 16


def _mlstm_kernel(*refs, rev, n_lat, n_ctx):
    if rev:
        (q_ref, qp_ref, qn_ref, k_ref, kp_ref, kn_ref, v_ref, gt_ref, cw_ref, cb_ref, gb_ref,
         og_ref, ng_ref, hf_ref, o_ref, c_s, n_s, m_s) = refs
    else:
        (q_ref, qp_ref, qn_ref, k_ref, kp_ref, kn_ref, v_ref, gt_ref, cw_ref, cb_ref, gb_ref,
         o_ref, c_s, n_s, m_s) = refs
    s = pl.program_id(1)
    L = ML_CHUNK
    tile = _scan_tile(s, n_lat, n_ctx, rev)
    prev_ok, next_ok = _halo_valid(tile, n_lat, n_ctx)

    @pl.when(s == 0)
    def _():
        c_s[...] = jnp.zeros_like(c_s)
        n_s[...] = jnp.zeros_like(n_s)
        m_s[...] = jnp.zeros_like(m_s)

    row = lax.broadcasted_iota(jnp.int32, (L, 1), 0)

    def conv_silu(x_ref, xp_ref, xn_ref, lanes):
        x = x_ref[...].astype(F32)
        xp = xp_ref[ML_HALO - 1:ML_HALO, :].astype(F32) * prev_ok
        xn0 = xn_ref[0:1, :].astype(F32) * next_ok
        xn1 = xn_ref[1:2, :].astype(F32) * next_ok
        x_m1 = jnp.where(row == 0, xp, pltpu.roll(x, 1, 0))
        x_p1 = jnp.where(row == L - 1, xn0, pltpu.roll(x, L - 1, 0))
        x_p2 = jnp.where(row == L - 1, xn1, jnp.where(row == L - 2, xn0, pltpu.roll(x, L - 2, 0)))
        y = (cb_ref[:, lanes] + cw_ref[0:1, lanes] * x_m1 + cw_ref[1:2, lanes] * x
             + cw_ref[2:3, lanes] * x_p1 + cw_ref[3:4, lanes] * x_p2)
        return y * jax.nn.sigmoid(y)

    qc = conv_silu(q_ref, qp_ref, qn_ref, slice(0, ML_WIDTH)).astype(BF16)
    kc = (conv_silu(k_ref, kp_ref, kn_ref, slice(ML_WIDTH, 2 * ML_WIDTH)) * (ML_DIM ** -0.5)).astype(BF16)
    gates = gt_ref[...] + gb_ref[...]

    ti = lax.broadcasted_iota(jnp.int32, (L, L), 0)
    si = lax.broadcasted_iota(jnp.int32, (L, L), 1)
    causal = (si >= ti) if rev else (si <= ti)
    nt = (((1,), (1,)), ((), ()))
    tn = (((0,), (0,)), ((), ()))
    dsel = 2 * ML_HEADS if rev else 0

    for h in range(ML_HEADS):
        lanes = slice(h * ML_DIM, (h + 1) * ML_DIM)
        q, k, v = qc[:, lanes], kc[:, lanes], v_ref[:, lanes]
        ig = gates[:, dsel + h:dsel + h + 1]
        fg = gates[:, dsel + ML_HEADS + h:dsel + ML_HEADS + h + 1]
        lf = jnp.minimum(fg, 0.0) - jnp.log1p(jnp.exp(-jnp.abs(fg)))
        m_prev = m_s[h][0:1, 0:1]

        lf_row = jnp.transpose(jnp.broadcast_to(lf, (L, L)))
        bcum = jnp.sum(jnp.where(causal, lf_row, 0.0), axis=-1, keepdims=True)
        e_row = jnp.transpose(jnp.broadcast_to(bcum - ig, (L, L)))
        dmat = jnp.where(causal, bcum - e_row, -jnp.inf)
        g_inter = bcum + m_prev
        m_t = jnp.maximum(g_inter, jnp.max(dmat, axis=-1, keepdims=True))
        w_inter = jnp.exp(g_inter - m_t)
        smat = lax.dot_general(q, k, nt, preferred_element_type=F32) * jnp.exp(dmat - m_t)
        cmat = c_s[h]
        nvec = n_s[h]
        num = (jnp.dot(smat.astype(BF16), v, preferred_element_type=F32)
               + w_inter * lax.dot_general(q, cmat.astype(BF16), nt, preferred_element_type=F32))
        den = (jnp.sum(smat, axis=-1, keepdims=True)
               + w_inter * jnp.sum(q.astype(F32) * nvec, axis=-1, keepdims=True))
        hout = num / jnp.maximum(jnp.abs(den), jnp.exp(-m_t))

        total = jnp.sum(lf, axis=0, keepdims=True)
        decay = total + m_prev
        w_s = total - bcum + ig
        m_new = jnp.maximum(decay, jnp.max(w_s, axis=0, keepdims=True))
        ws = jnp.exp(w_s - m_new)
        sc = jnp.exp(decay - m_new)
        wv = (ws * v.astype(F32)).astype(BF16)
        c_s[h] = sc * cmat + lax.dot_general(wv, k, tn, preferred_element_type=F32)
        n_s[h] = sc * nvec + jnp.sum(ws * k.astype(F32), axis=0, keepdims=True)
        m_s[h] = jnp.broadcast_to(m_new, m_s.shape[1:])

        if rev:
            hsum = hout + hf_ref[:, lanes]
            ms = jnp.mean(hsum * hsum, axis=-1, keepdims=True)
            hn = hsum * lax.rsqrt(ms + EPS) * ng_ref[:, lanes]
            o_ref[:, lanes] = (hn * jax.nn.sigmoid(og_ref[:, lanes].astype(F32))).astype(o_ref.dtype)
        else:
            o_ref[:, lanes] = hout


def _mlstm(ub, uf, conv_w, conv_b, gate_b, norm_g, l, d, hf, *, bsz, npb, n_lat, n_ctx):
    t = ub.shape[0]
    L = ML_CHUNK
    rev = d == 1
    tile = functools.partial(_scan_tile, n_lat=n_lat, n_ctx=n_ctx, rev=rev)
    wpb = npb // ML_WIDTH
    hpb = L // ML_HALO

    def main(off):
        return pl.BlockSpec((L, ML_WIDTH), lambda b, s: (tile(s), b * wpb + off // ML_WIDTH))

    def prev(off):
        return pl.BlockSpec((ML_HALO, ML_WIDTH),
                            lambda b, s: (jnp.maximum(tile(s) * hpb - 1, 0), b * wpb + off // ML_WIDTH))

    def nxt(off):
        return pl.BlockSpec((ML_HALO, ML_WIDTH),
                            lambda b, s: (jnp.minimum((tile(s) + 1) * hpb, t // ML_HALO - 1), b * wpb + off // ML_WIDTH))

    in_specs = [
        main(OFF_MQ), prev(OFF_MQ), nxt(OFF_MQ), main(OFF_MK), prev(OFF_MK), nxt(OFF_MK), main(OFF_MV),
        pl.BlockSpec((L, GATE_PAD), lambda b, s: (tile(s), b)),
        pl.BlockSpec((None, CONV_W, 2 * ML_WIDTH), lambda b, s: (l, 0, 0)),
        pl.BlockSpec((None, 1, 2 * ML_WIDTH), lambda b, s: (l, 0, 0)),
        pl.BlockSpec((None, 1, GATE_PAD), lambda b, s: (l, 0, 0)),
    ]
    args = [ub, ub, ub, ub, ub, ub, ub, uf, conv_w, conv_b, gate_b]
    if rev:
        in_specs += [
            main(OFF_MO),
            pl.BlockSpec((None, 1, ML_WIDTH), lambda b, s: (l, 0, 0)),
            pl.BlockSpec((L, ML_WIDTH), lambda b, s: (tile(s), b)),
        ]
        args += [ub, norm_g, hf]
    return pl.pallas_call(
        functools.partial(_mlstm_kernel, rev=rev, n_lat=n_lat, n_ctx=n_ctx),
        grid=(bsz, n_lat + n_ctx),
        in_specs=in_specs,
        out_specs=pl.BlockSpec((L, ML_WIDTH), lambda b, s: (tile(s), b)),
        out_shape=jax.ShapeDtypeStruct((t, bsz * ML_WIDTH), BF16 if rev else F32),
        scratch_shapes=[
            pltpu.VMEM((ML_HEADS, ML_DIM, ML_DIM), F32),
            pltpu.VMEM((ML_HEADS, 1, ML_DIM), F32),
            pltpu.VMEM((ML_HEADS, SUBLANE, LANE), F32),
        ],
        compiler_params=_cparams(("arbitrary", "arbitrary")),
        name="mlstm_rev" if rev else "mlstm_fwd",
    )(*args)


def _rope_tables(seq, ctx_len):
    rows = seq // GRID_W
    row = jnp.repeat(jnp.arange(rows, dtype=F32), GRID_W)
    col = jnp.tile(jnp.arange(GRID_W, dtype=F32), rows)
    quarter = DA_DIM // 4
    inv = ROPE_BASE ** (-jnp.arange(quarter, dtype=F32) / quarter)
    ar, ac = row[:, None] * inv, col[:, None] * inv
    ang = jnp.concatenate([ar, ar, ac, ac], axis=-1)
    cos, sin = jnp.cos(ang), jnp.sin(ang)
    lane = jnp.arange(DA_DIM) // quarter
    sin_a = jnp.where(lane % 2 == 0, -sin, 0.0)
    sin_b = jnp.where(lane % 2 == 1, sin, 0.0)
    pad = lambda tbl, fill: jnp.concatenate([tbl, jnp.full((ctx_len, DA_DIM), fill, F32)], axis=0)
    return pad(cos, 1.0), pad(sin_a, 0.0), pad(sin_b, 0.0)


def _row_tile(t, parts, seq):
    tm = t // parts
    assert tm * parts == t and tm % ROW_CHUNK == 0 and seq % ROW_CHUNK == 0, (t, parts)
    return tm


def kernel(x, c, ctx, c_ctx, w_mod, b_mod, norm1_g, norm2_g, w_in, attn_qnorm_g, attn_knorm_g, attn_lambda,
           attn_subln_g, rnn_conv_w, rnn_conv_b, rnn_wa, rnn_ba, rnn_wx, rnn_bx, rnn_lambda, ml_conv_w, ml_conv_b,
           ml_gate_b, ml_norm_g, w_branch_attn, w_branch_rnn, w_branch_ml, w_out, w_ffn1, w_ffn3, w_ffn2):
    bsz, seq, d = x.shape
    ctx_len = ctx.shape[1]
    depth = w_in.shape[0]
    t = seq + ctx_len
    assert bsz == SUBLANE and seq % 256 == 0 and ctx_len % 256 == 0 and d % TN_PROJ == 0

    cuts = [0]
    for w in (DA_WIDTH, DA_WIDTH, DA_WIDTH, RNN_WIDTH, RNN_WIDTH, ML_WIDTH, ML_WIDTH, ML_WIDTH, ML_WIDTH, ML_GATES,
              N_BRANCH * d):
        cuts.append(cuts[-1] + w)
    seg = lambda k: w_in[:, :, cuts[k]:cuts[k + 1]]
    w_b = jnp.concatenate([seg(0), seg(1), seg(2), seg(5), seg(6), seg(7), seg(8), seg(3), seg(4), seg(10)],
                          axis=-1).astype(BF16)
    npb = w_b.shape[-1]
    w_mg = jnp.pad(seg(9), ((0, 0), (0, 0), (0, GATE_PAD - ML_GATES))).astype(BF16)
    assert npb % ML_WIDTH == 0
    qkg = jnp.stack([attn_qnorm_g, attn_knorm_g], axis=1)
    rope = _rope_tables(seq, ctx_len)
    w_gate = jnp.concatenate([rnn_wa, rnn_wx], axis=-1).astype(BF16)
    rg_bias = jnp.concatenate([rnn_ba[:, :, None], rnn_bx[:, :, None], rnn_lambda[:, :, None],
                               jnp.zeros((depth, 2, SUBLANE - 3, RNN_WIDTH), F32)], axis=2)
    gate_b = jnp.pad(ml_gate_b, ((0, 0), (0, GATE_PAD - ML_GATES)))[:, None, :]
    wba, wbr, wbm, wo = (w.astype(BF16) for w in (w_branch_attn, w_branch_rnn, w_branch_ml, w_out))
    w1, w3, w2 = (w.astype(BF16) for w in (w_ffn1, w_ffn3, w_ffn2))
    row3 = lambda a: a[:, None, :]

    cvecs = jnp.concatenate([c, c_ctx[None, :], jnp.zeros((2 * SUBLANE - bsz - 1, d), F32)], axis=0)
    mod_all = _modvec(cvecs, w_mod, b_mod)[:, :bsz + 1]
    mod_all = jnp.pad(mod_all.reshape(depth, bsz + 1, N_MOD, d), ((0, 0), (0, 0), (0, MOD_ROWS - N_MOD), (0, 0)))

    h = jnp.concatenate([x, ctx], axis=1)
    tm_proj = _row_tile(t, 4, seq)
    tm_half = _row_tile(t, 8, seq)
    n_lat_rg, n_ctx_rg = seq // RG_TS, ctx_len // RG_TS
    n_lat_ml, n_ctx_ml = seq // ML_CHUNK, ctx_len // ML_CHUNK

    for l in range(depth):
        lam_init = 0.8 - 0.6 * math.exp(-0.3 * l)
        mod = mod_all[l]
        ub, uf = _inproj(h, mod, row3(norm1_g), w_b, w_mg, l, qkg, rope, seq=seq, tm=tm_proj, tn=TN_PROJ)
        a = _attention(ub, attn_lambda, row3(attn_subln_g), l, bsz=bsz, seq=seq, npb=npb, tq=256, lam_init=lam_init)
        uf3 = ub.reshape(t, bsz, npb)[:, :, OFF_RX:OFF_RX + 2 * RNN_WIDTH].astype(F32)
        hf3 = _rglru(uf3, rnn_conv_w, row3(rnn_conv_b), w_gate, rg_bias, l, 0, None, n_lat=n_lat_rg, n_ctx=n_ctx_rg)
        r = _rglru(uf3, rnn_conv_w, row3(rnn_conv_b), w_gate, rg_bias, l, 1, hf3, n_lat=n_lat_rg, n_ctx=n_ctx_rg)
        r = r.astype(BF16).reshape(t, bsz * RNN_WIDTH)
        mf = _mlstm(ub, uf, ml_conv_w, row3(ml_conv_b), gate_b, row3(ml_norm_g), l, 0, None, bsz=bsz, npb=npb,
                    n_lat=n_lat_ml, n_ctx=n_ctx_ml)
        m = _mlstm(ub, uf, ml_conv_w, row3(ml_conv_b), gate_b, row3(ml_norm_g), l, 1, mf, bsz=bsz, npb=npb,
                   n_lat=n_lat_ml, n_ctx=n_ctx_ml)
        h = _merge(h, mod, a, r, m, ub, wba, wbr, wbm, wo, l, seq=seq, npb=npb, tm=tm_half, tz=TN_PROJ)
        h = _ffn(h, mod, row3(norm2_g), w1, w3, w2, l, seq=seq, tm=tm_half, tf=TN_PROJ)
    return h[:, :seq, :]
```

```python
import functools
import math

import jax
import jax.numpy as jnp
from jax import lax
from jax.experimental import pallas as pl
from jax.experimental.pallas import tpu as pltpu

F32 = jnp.float32
BF16 = jnp.bfloat16

EPS = 1e-6
GRID_W = 64
ROPE_BASE = 10000.0
N_MOD = 6
N_BRANCH = 3
MOD_ROWS = 8

DA_HEADS = 4
DA_DIM = 128
DA_WIDTH = DA_HEADS * 2 * DA_DIM
DA_SCALE = DA_DIM ** -0.5

RNN_WIDTH = 1024
RNN_BLOCKS = 8
RNN_BLOCK = RNN_WIDTH // RNN_BLOCKS
LRU_C = 8.0
CONV_W = 4

ML_HEADS = 4
ML_DIM = 256
ML_WIDTH = ML_HEADS * ML_DIM
ML_GATES = 2 * 2 * ML_HEADS
ML_CHUNK = 128

LANE = 128
SUBLANE = 8
VMEM_LIMIT = 56 * 1024 * 1024

OFF_AQ, OFF_AK, OFF_AV = 0, DA_WIDTH, 2 * DA_WIDTH
OFF_MQ = 3 * DA_WIDTH
OFF_MK, OFF_MV, OFF_MO = OFF_MQ + ML_WIDTH, OFF_MQ + 2 * ML_WIDTH, OFF_MQ + 3 * ML_WIDTH
OFF_BG = OFF_MQ + 4 * ML_WIDTH
GATE_PAD = LANE

TN_PROJ = 512
ROW_CHUNK = 16


def _cparams(sem):
    return pltpu.CompilerParams(dimension_semantics=sem, vmem_limit_bytes=VMEM_LIMIT)


def _modvec_kernel(s_ref, w_ref, b_ref, o_ref):
    s = s_ref[...]
    s = s * jax.nn.sigmoid(s)
    o_ref[...] = jnp.dot(s.astype(BF16), w_ref[...].astype(BF16), preferred_element_type=F32) + b_ref[...]


def _modvec(cvecs, w_mod, b_mod):
    depth, d, nm = w_mod.shape
    rows = cvecs.shape[0]
    tn = 1024
    return pl.pallas_call(
        _modvec_kernel,
        grid=(depth, nm // tn),
        in_specs=[
            pl.BlockSpec((rows, d), lambda l, j: (0, 0)),
            pl.BlockSpec((None, d, tn), lambda l, j: (l, 0, j)),
            pl.BlockSpec((None, 1, tn), lambda l, j: (l, 0, j)),
        ],
        out_specs=pl.BlockSpec((None, rows, tn), lambda l, j: (l, 0, j)),
        out_shape=jax.ShapeDtypeStruct((depth, rows, nm), F32),
        compiler_params=_cparams(("arbitrary", "arbitrary")),
        name="modvec",
    )(cvecs, w_mod, b_mod.reshape(depth, 1, nm))


def _norm_modulate(h_ref, ml_ref, mc_ref, g_ref, xn_ref, *, row0, seq, shift_row):
    tm = xn_ref.shape[0]
    g = g_ref[...]
    sh_l, sc_l = ml_ref[shift_row:shift_row + 1, :], ml_ref[shift_row + 1:shift_row + 2, :]
    sh_c, sc_c = mc_ref[shift_row:shift_row + 1, :], mc_ref[shift_row + 1:shift_row + 2, :]

    def body(c, carry):
        r0 = pl.multiple_of(c * ROW_CHUNK, ROW_CHUNK)
        x = h_ref[pl.ds(r0, ROW_CHUNK), :]
        ms = jnp.mean(x * x, axis=-1, keepdims=True)
        y = x * lax.rsqrt(ms + EPS) * g
        is_ctx = (row0 + r0) >= seq
        sh = jnp.where(is_ctx, sh_c, sh_l)
        sc = jnp.where(is_ctx, sc_c, sc_l)
        xn_ref[pl.ds(r0, ROW_CHUNK), :] = (y * (1.0 + sc) + sh).astype(BF16)
        return carry

    lax.fori_loop(0, tm // ROW_CHUNK, body, 0, unroll=2)


def _gated_residual(h_ref, acc_ref, ml_ref, mc_ref, o_ref, *, row0, seq, gate_row):
    tm = acc_ref.shape[0]
    gt_l = ml_ref[gate_row:gate_row + 1, :]
    gt_c = mc_ref[gate_row:gate_row + 1, :]

    def body(c, carry):
        r0 = pl.multiple_of(c * ROW_CHUNK, ROW_CHUNK)
        gt = jnp.where((row0 + r0) >= seq, gt_c, gt_l)
        o_ref[pl.ds(r0, ROW_CHUNK), :] = h_ref[pl.ds(r0, ROW_CHUNK), :] + gt * acc_ref[pl.ds(r0, ROW_CHUNK), :]
        return carry

    lax.fori_loop(0, tm // ROW_CHUNK, body, 0, unroll=2)


EPILOGUE_ROWS_CAP = 272


def _epilogue_rows(tm):
    return max(r for r in range(ROW_CHUNK, min(tm, EPILOGUE_ROWS_CAP) + 1, ROW_CHUNK) if tm % r == 0)


def _inproj_kernel(h_ref, ml_ref, mc_ref, g_ref, w_ref, wg_ref, qkg_ref, cos_ref, sn_ref, u_ref, r_ref, gt_ref,
                   xn_ref, acc_ref, *, seq, n_q_tiles, n_k_tiles, n_u_tiles):
    i, j = pl.program_id(1), pl.program_id(2)
    tm, tn = acc_ref.shape

    @pl.when(j == 0)
    def _():
        _norm_modulate(h_ref, ml_ref, mc_ref, g_ref, xn_ref, row0=i * tm, seq=seq, shift_row=0)
        gt_ref[...] = jnp.dot(xn_ref[...], wg_ref[...], preferred_element_type=F32)

    def qk_epilogue(gain, scale):
        rc = _epilogue_rows(tm)

        def body(c, carry):
            r0 = pl.multiple_of(c * rc, rc)
            cs, sn = cos_ref[pl.ds(r0, rc), :], sn_ref[pl.ds(r0, rc), :]
            for gidx in range(tn // DA_DIM):
                lanes = slice(gidx * DA_DIM, (gidx + 1) * DA_DIM)
                x = acc_ref[pl.ds(r0, rc), lanes]
                ms = jnp.mean(x * x, axis=-1, keepdims=True)
                y = x * lax.rsqrt(ms + EPS) * gain
                y = y * cs + pltpu.roll(y, DA_DIM // 2, 1) * sn
                u_ref[pl.ds(r0, rc), lanes] = (y * scale).astype(u_ref.dtype)
            return carry

        lax.fori_loop(0, tm // rc, body, 0)

    @pl.when(j < n_q_tiles)
    def _():
        acc_ref[...] = jnp.dot(xn_ref[...], w_ref[...], preferred_element_type=F32)
        qk_epilogue(qkg_ref[0:1, :], DA_SCALE * math.log2(math.e))

    @pl.when(jnp.logical_and(j >= n_q_tiles, j < n_q_tiles + n_k_tiles))
    def _():
        acc_ref[...] = jnp.dot(xn_ref[...], w_ref[...], preferred_element_type=F32)
        qk_epilogue(qkg_ref[1:2, :], 1.0)

    @pl.when(jnp.logical_and(j >= n_q_tiles + n_k_tiles, j < n_u_tiles))
    def _():
        u_ref[...] = jnp.dot(xn_ref[...], w_ref[...], preferred_element_type=F32).astype(u_ref.dtype)

    @pl.when(j >= n_u_tiles)
    def _():
        r_ref[...] = jnp.dot(xn_ref[...], w_ref[...], preferred_element_type=F32).astype(r_ref.dtype)


def _inproj(h, mod, norm_g, w, w_gate, l, qkg, rope, *, seq, tm, tn):
    bsz, t, d = h.shape
    nr = 2 * RNN_WIDTH // tn
    nu = w.shape[1] - nr
    n_q = DA_WIDTH // tn
    cos, sn = rope
    kern = functools.partial(_inproj_kernel, seq=seq, n_q_tiles=n_q, n_k_tiles=n_q, n_u_tiles=nu)
    return pl.pallas_call(
        kern,
        grid=(bsz, t // tm, nu + nr),
        in_specs=[
            pl.BlockSpec((None, tm, d), lambda b, i, j: (b, i, 0)),
            pl.BlockSpec((None, MOD_ROWS, d), lambda b, i, j: (b, 0, 0)),
            pl.BlockSpec((None, MOD_ROWS, d), lambda b, i, j: (bsz, 0, 0)),
            pl.BlockSpec((None, 1, d), lambda b, i, j: (l, 0, 0)),
            pl.BlockSpec((None, None, d, tn), lambda b, i, j: (l, j, 0, 0)),
            pl.BlockSpec((None, d, GATE_PAD), lambda b, i, j: (l, 0, 0)),
            pl.BlockSpec((None, 2, DA_DIM), lambda b, i, j: (l, 0, 0)),
            pl.BlockSpec((tm, DA_DIM), lambda b, i, j: (i, 0)),
            pl.BlockSpec((tm, DA_DIM), lambda b, i, j: (i, 0)),
        ],
        out_specs=[
            pl.BlockSpec((tm, tn), lambda b, i, j: (i, b * nu + jnp.minimum(j, nu - 1))),
            pl.BlockSpec((tm, tn), lambda b, i, j: (i, b * nr + jnp.clip(j - nu, 0, nr - 1))),
            pl.BlockSpec((tm, GATE_PAD), lambda b, i, j: (i, b)),
        ],
        out_shape=[
            jax.ShapeDtypeStruct((t, bsz * nu * tn), BF16),
            jax.ShapeDtypeStruct((t, bsz * nr * tn), BF16),
            jax.ShapeDtypeStruct((t, bsz * GATE_PAD), F32),
        ],
        scratch_shapes=[pltpu.VMEM((tm, d), BF16), pltpu.VMEM((tm, tn), F32)],
        compiler_params=_cparams(("arbitrary", "arbitrary", "arbitrary")),
        name="inproj",
    )(h, mod, mod, norm_g, w, w_gate, qkg, cos, sn)


ATT_KB = 256


def _attn_kernel(lam_ref, q_ref, k_ref, v_ref, g_ref, o_ref, s0_s, s1_s, p0_s, p1_s, *, seq, lam_init, n_lat_tiles):
    qi = pl.program_id(2)
    tq = q_ref.shape[0]
    n_chunks = k_ref.shape[0] // ATT_KB
    s_bufs, p_bufs = (s0_s, s1_s), (p0_s, p1_s)
    lv = lam_ref[...]
    lam = (jnp.exp(jnp.sum(lv[0:1] * lv[1:2], axis=-1, keepdims=True))
           - jnp.exp(jnp.sum(lv[2:3] * lv[3:4], axis=-1, keepdims=True)) + lam_init)
    nt = (((1,), (1,)), ((), ()))

    def lane_groups(x):
        return [x[:, g * LANE:(g + 1) * LANE] for g in range(x.shape[1] // LANE)]

    def keys_of(c):
        return slice(c * ATT_KB, (c + 1) * ATT_KB)

    def score_chunk(mp, c, q, m_part):
        lanes = slice(mp * DA_DIM, (mp + 1) * DA_DIM)
        s = lax.dot_general(q, k_ref[keys_of(c), lanes], nt, preferred_element_type=F32)
        s_bufs[mp][:, keys_of(c)] = s
        for grp in lane_groups(s):
            m_part = jnp.maximum(m_part, grp)
        return m_part

    def exp_chunk(mp, c, m, l_part):
        p = jnp.exp2(s_bufs[mp][:, keys_of(c)] - m)
        for grp in lane_groups(p):
            l_part = l_part + grp
        p_bufs[mp][:, keys_of(c)] = p.astype(BF16)
        return l_part

    def pv_chunk(mp, c, acc):
        part = jnp.dot(p_bufs[mp][:, keys_of(c)], v_ref[keys_of(c), :], preferred_element_type=F32)
        return part if acc is None else acc + part

    def attend(c_lo, c_hi):
        chunks = range(c_lo, c_hi)
        neg_inf = jnp.full((tq, LANE), -jnp.inf, F32)
        zeros = jnp.zeros((tq, LANE), F32)
        m0_part = neg_inf
        q0 = q_ref[:, :DA_DIM]
        for c in chunks:
            m0_part = score_chunk(0, c, q0, m0_part)
        m0 = jnp.max(m0_part, axis=-1, keepdims=True)
        q1 = q_ref[:, DA_DIM:]
        m1_part, l0_part = neg_inf, zeros
        for c in chunks:
            m1_part = score_chunk(1, c, q1, m1_part)
            l0_part = exp_chunk(0, c, m0, l0_part)
        m1 = jnp.max(m1_part, axis=-1, keepdims=True)
        l0 = jnp.sum(l0_part, axis=-1, keepdims=True)
        l1_part, o0 = zeros, None
        for c in chunks:
            l1_part = exp_chunk(1, c, m1, l1_part)
            o0 = pv_chunk(0, c, o0)
        l1 = jnp.sum(l1_part, axis=-1, keepdims=True)
        span = slice(c_lo * ATT_KB, c_hi * ATT_KB)
        o1 = jnp.dot(p1_s[:, span], v_ref[span, :], preferred_element_type=F32)
        o = o0 * (1.0 / l0) - o1 * (lam / l1)
        ms = jnp.mean(o * o, axis=-1, keepdims=True)
        o_ref[...] = (o * lax.rsqrt(ms + EPS) * g_ref[...] * (1.0 - lam_init)).astype(o_ref.dtype)

    @pl.when(qi < n_lat_tiles)
    def _():
        attend(0, n_chunks)

    @pl.when(qi >= n_lat_tiles)
    def _():
        attend(seq // ATT_KB, n_chunks)


def _attention(ub, attn_lambda, subln_g, l, *, bsz, seq, npb, tq, lam_init):
    t = ub.shape[0]
    hw = 2 * DA_DIM
    cpb = npb // hw
    kern = functools.partial(_attn_kernel, seq=seq, lam_init=lam_init, n_lat_tiles=seq // tq)
    return pl.pallas_call(
        kern,
        grid=(bsz, DA_HEADS, t // tq),
        in_specs=[
            pl.BlockSpec((None, 4, DA_DIM), lambda b, h, qi: (l, 0, 0)),
            pl.BlockSpec((tq, hw), lambda b, h, qi: (qi, b * cpb + OFF_AQ // hw + h)),
            pl.BlockSpec((t, hw), lambda b, h, qi: (0, b * cpb + OFF_AK // hw + h)),
            pl.BlockSpec((t, hw), lambda b, h, qi: (0, b * cpb + OFF_AV // hw + h)),
            pl.BlockSpec((None, 1, hw), lambda b, h, qi: (l, 0, 0)),
        ],
        out_specs=pl.BlockSpec((tq, hw), lambda b, h, qi: (qi, b * DA_HEADS + h)),
        out_shape=jax.ShapeDtypeStruct((t, bsz * DA_WIDTH), BF16),
        scratch_shapes=[pltpu.VMEM((tq, t), F32), pltpu.VMEM((tq, t), F32),
                        pltpu.VMEM((tq, t), BF16), pltpu.VMEM((tq, t), BF16)],
        compiler_params=_cparams(("arbitrary", "arbitrary", "arbitrary")),
        name="diff_attention",
    )(attn_lambda, ub, ub, ub, subln_g)


def _merge_kernel(h_ref, ml_ref, mc_ref, a_ref, r_ref, m_ref, ga_ref, gr_ref, gm_ref, wa_ref, wr_ref, wm_ref, wo_ref,
                  o_ref, acc_ref, *, seq):
    i, j = pl.program_id(1), pl.program_id(2)
    tm = acc_ref.shape[0]

    def branch(x_ref, g_ref, w_ref):
        y = jnp.dot(x_ref[...].astype(BF16), w_ref[...], preferred_element_type=F32)
        return jax.nn.sigmoid(g_ref[...].astype(F32)) * y

    z = branch(a_ref, ga_ref, wa_ref) + branch(r_ref, gr_ref, wr_ref) + branch(m_ref, gm_ref, wm_ref)
    part = jnp.dot(z.astype(BF16), wo_ref[...], preferred_element_type=F32)

    @pl.when(j == 0)
    def _():
        acc_ref[...] = part

    @pl.when(j > 0)
    def _():
        acc_ref[...] += part

    @pl.when(j == pl.num_programs(2) - 1)
    def _():
        _gated_residual(h_ref, acc_ref, ml_ref, mc_ref, o_ref, row0=i * tm, seq=seq, gate_row=2)


def _merge(h, mod, a, r, m, ub, wba, wbr, wbm, wo, l, *, seq, npb, rows, tm, tz):
    bsz, _, d = h.shape
    nz = d // tz
    gpb = npb // tz
    g0 = OFF_BG // tz

    def gate_spec(k):
        return pl.BlockSpec((tm, tz), lambda b, i, j: (i, b * gpb + g0 + k * nz + j))

    def branch_spec(width):
        return pl.BlockSpec((tm, width), lambda b, i, j: (i, b))

    def w_spec(width):
        return pl.BlockSpec((None, None, width, tz), lambda b, i, j: (l, j, 0, 0))

    return pl.pallas_call(
        functools.partial(_merge_kernel, seq=seq),
        grid=(bsz, rows // tm, nz),
        in_specs=[
            pl.BlockSpec((None, tm, d), lambda b, i, j: (b, i, 0)),
            pl.BlockSpec((None, MOD_ROWS, d), lambda b, i, j: (b, 0, 0)),
            pl.BlockSpec((None, MOD_ROWS, d), lambda b, i, j: (bsz, 0, 0)),
            branch_spec(DA_WIDTH), branch_spec(RNN_WIDTH), branch_spec(ML_WIDTH),
            gate_spec(0), gate_spec(1), gate_spec(2),
            w_spec(DA_WIDTH), w_spec(RNN_WIDTH), w_spec(ML_WIDTH),
            pl.BlockSpec((None, tz, d), lambda b, i, j: (l, j, 0)),
        ],
        out_specs=pl.BlockSpec((None, tm, d), lambda b, i, j: (b, i, 0)),
        out_shape=jax.ShapeDtypeStruct((bsz, rows, d), F32),
        scratch_shapes=[pltpu.VMEM((tm, d), F32)],
        compiler_params=_cparams(("arbitrary", "arbitrary", "arbitrary")),
        name="merge_out",
    )(h, mod, mod, a, r, m, ub, ub, ub, wba, wbr, wbm, wo)


def _ffn_kernel(h_ref, ml_ref, mc_ref, g_ref, w1_ref, w3_ref, w2_ref, o_ref, xn_ref, acc_ref, *, seq):
    i, j = pl.program_id(1), pl.program_id(2)
    tm = acc_ref.shape[0]

    @pl.when(j == 0)
    def _():
        _norm_modulate(h_ref, ml_ref, mc_ref, g_ref, xn_ref, row0=i * tm, seq=seq, shift_row=3)

    xn = xn_ref[...]
    u = jnp.dot(xn, w1_ref[...], preferred_element_type=F32)
    v = jnp.dot(xn, w3_ref[...], preferred_element_type=F32)
    part = jnp.dot((u * jax.nn.sigmoid(u) * v).astype(BF16), w2_ref[...], preferred_element_type=F32)

    @pl.when(j == 0)
    def _():
        acc_ref[...] = part

    @pl.when(j > 0)
    def _():
        acc_ref[...] += part

    @pl.when(j == pl.num_programs(2) - 1)
    def _():
        _gated_residual(h_ref, acc_ref, ml_ref, mc_ref, o_ref, row0=i * tm, seq=seq, gate_row=5)


def _ffn(h, mod, norm_g, w1, w3, w2, l, *, seq, tm):
    bsz, rows, d = h.shape
    nf, tf = w1.shape[1], w1.shape[3]
    return pl.pallas_call(
        functools.partial(_ffn_kernel, seq=seq),
        grid=(bsz, rows // tm, nf),
        in_specs=[
            pl.BlockSpec((None, tm, d), lambda b, i, j: (b, i, 0)),
            pl.BlockSpec((None, MOD_ROWS, d), lambda b, i, j: (b, 0, 0)),
            pl.BlockSpec((None, MOD_ROWS, d), lambda b, i, j: (bsz, 0, 0)),
            pl.BlockSpec((None, 1, d), lambda b, i, j: (l, 0, 0)),
            pl.BlockSpec((None, None, d, tf), lambda b, i, j: (l, j, 0, 0)),
            pl.BlockSpec((None, None, d, tf), lambda b, i, j: (l, j, 0, 0)),
            pl.BlockSpec((None, tf, d), lambda b, i, j: (l, j, 0)),
        ],
        out_specs=pl.BlockSpec((None, tm, d), lambda b, i, j: (b, i, 0)),
        out_shape=jax.ShapeDtypeStruct(h.shape, F32),
        scratch_shapes=[pltpu.VMEM((tm, d), BF16), pltpu.VMEM((tm, d), F32)],
        compiler_params=_cparams(("arbitrary", "arbitrary", "arbitrary")),
        name="swiglu_ffn",
    )(h, mod, mod, norm_g, w1, w3, w2)


def _scan_tile(s, n_lat, n_ctx, rev):
    if rev:
        return jnp.where(s < n_ctx, n_lat + n_ctx - 1 - s, n_lat - 1 - (s - n_ctx))
    return jnp.where(s < n_ctx, n_lat + s, s - n_ctx)


def _halo_valid(tile, n_lat, n_ctx):
    prev_ok = jnp.logical_and(tile != 0, tile != n_lat)
    next_ok = jnp.logical_and(tile != n_lat - 1, tile != n_lat + n_ctx - 1)
    return prev_ok.astype(F32), next_ok.astype(F32)


def _softplus(x):
    return jnp.maximum(x, 0.0) + jnp.log1p(jnp.exp(-jnp.abs(x)))


RG_CB = 512
RG_TS = 128
RG_TC = 32


def _rglru_kernel(*refs, rev, n_lat, n_ctx):
    if rev:
        x_ref, xp_ref, xn_ref, cw_ref, cb_ref, w_ref, bias_ref, g_ref, hf_ref, o_ref, xw_s, a_s, b_s, h_s = refs
    else:
        x_ref, xp_ref, xn_ref, cw_ref, cb_ref, w_ref, bias_ref, o_ref, xw_s, a_s, b_s, h_s = refs
    s = pl.program_id(1)
    ts = x_ref.shape[0]
    tile = _scan_tile(s, n_lat, n_ctx, rev)
    prev_ok, next_ok = _halo_valid(tile, n_lat, n_ctx)

    @pl.when(s == 0)
    def _():
        h_s[...] = jnp.zeros_like(h_s)

    xw_s[0:1] = xp_ref[...] * prev_ok
    xw_s[1:ts + 1] = x_ref[...]
    xw_s[ts + 1:ts + 3] = xn_ref[...] * next_ok

    ba, bx = bias_ref[0:1, :], bias_ref[1:2, :]
    sp = _softplus(-bias_ref[2:3, :])
    cw = [cw_ref[k:k + 1, :] for k in range(CONV_W)]
    cb = cb_ref[...]

    def gates(c, carry):
        t0 = pl.multiple_of(c * RG_TC, RG_TC)
        conv = cb + sum(cw[k] * xw_s[pl.ds(t0 + k, RG_TC)] for k in range(CONV_W))
        x2 = conv.reshape(RG_TC * SUBLANE, RG_CB)
        for blk in range(RG_CB // RNN_BLOCK):
            lanes = slice(blk * RNN_BLOCK, (blk + 1) * RNN_BLOCK)
            xb = x2[:, lanes]
            z = jnp.dot(xb.astype(BF16), w_ref[blk], preferred_element_type=F32)
            r = jax.nn.sigmoid(z[:, :RNN_BLOCK] + ba[:, lanes])
            ig = jax.nn.sigmoid(z[:, RNN_BLOCK:] + bx[:, lanes])
            log_a = -LRU_C * r * sp[:, lanes]
            a = jnp.exp(log_a)
            bb = jnp.sqrt(1.0 - a * a) * (ig * xb)
            a_s[pl.ds(t0, RG_TC), :, lanes] = a.reshape(RG_TC, SUBLANE, RNN_BLOCK)
            b_s[pl.ds(t0, RG_TC), :, lanes] = bb.reshape(RG_TC, SUBLANE, RNN_BLOCK)
        return carry

    lax.fori_loop(0, ts // RG_TC, gates, 0)

    def step(k, h):
        tt = (ts - 1 - k) if rev else k
        h = a_s[tt] * h + b_s[tt]
        if rev:
            o_ref[tt] = (h + hf_ref[tt]) * jax.nn.gelu(g_ref[tt])
        else:
            o_ref[tt] = h
        return h

    h_s[...] = lax.fori_loop(0, ts, step, h_s[...], unroll=8)


def _rglru(uf3, conv_w, conv_b, w_gate, bias, l, d, hf3, *, n_lat, n_ctx):
    t, bsz, _ = uf3.shape
    ts = RG_TS
    rev = d == 1
    tile = functools.partial(_scan_tile, n_lat=n_lat, n_ctx=n_ctx, rev=rev)
    nblk = RG_CB // RNN_BLOCK
    in_specs = [
        pl.BlockSpec((ts, bsz, RG_CB), lambda kc, s: (tile(s), 0, kc)),
        pl.BlockSpec((1, bsz, RG_CB), lambda kc, s: (jnp.maximum(tile(s) * ts - 1, 0), 0, kc)),
        pl.BlockSpec((2, bsz, RG_CB),
                     lambda kc, s: (jnp.minimum((tile(s) + 1) * (ts // 2), t // 2 - 1), 0, kc)),
        pl.BlockSpec((None, CONV_W, RG_CB), lambda kc, s: (l, 0, kc)),
        pl.BlockSpec((None, 1, RG_CB), lambda kc, s: (l, 0, kc)),
        pl.BlockSpec((None, None, nblk, RNN_BLOCK, 2 * RNN_BLOCK), lambda kc, s: (l, d, kc, 0, 0)),
        pl.BlockSpec((None, None, SUBLANE, RG_CB), lambda kc, s: (l, d, 0, kc)),
    ]
    args = [uf3, uf3, uf3, conv_w, conv_b, w_gate, bias]
    if rev:
        in_specs += [
            pl.BlockSpec((ts, bsz, RG_CB), lambda kc, s: (tile(s), 0, RNN_WIDTH // RG_CB + kc)),
            pl.BlockSpec((ts, bsz, RG_CB), lambda kc, s: (tile(s), 0, kc)),
        ]
        args += [uf3, hf3]
    return pl.pallas_call(
        functools.partial(_rglru_kernel, rev=rev, n_lat=n_lat, n_ctx=n_ctx),
        grid=(RNN_WIDTH // RG_CB, n_lat + n_ctx),
        in_specs=in_specs,
        out_specs=pl.BlockSpec((ts, bsz, RG_CB), lambda kc, s: (tile(s), 0, kc)),
        out_shape=jax.ShapeDtypeStruct((t, bsz, RNN_WIDTH), F32),
        scratch_shapes=[
            pltpu.VMEM((ts + CONV_W - 1, bsz, RG_CB), F32),
            pltpu.VMEM((ts, bsz, RG_CB), F32),
            pltpu.VMEM((ts, bsz, RG_CB), F32),
            pltpu.VMEM((bsz, RG_CB), F32),
        ],
        compiler_params=_cparams(("arbitrary", "arbitrary")),
        name="rglru_rev" if rev else "rglru_fwd",
    )(*args)


ML_HALO = 16


def _mlstm_kernel(*refs, rev, n_lat, n_ctx):
    if rev:
        qc_ref, kc_ref, v_ref, gt_ref, gb_ref, og_ref, ng_ref, hf_ref, o_ref, c_s, n_s, m_s = refs
    else:
        (q_ref, qp_ref, qn_ref, k_ref, kp_ref, kn_ref, v_ref, gt_ref, cw_ref, cb_ref, gb_ref,
         o_ref, qc_ref, kc_ref, c_s, n_s, m_s) = refs
    s = pl.program_id(1)
    L = ML_CHUNK
    tile = _scan_tile(s, n_lat, n_ctx, rev)
    prev_ok, next_ok = _halo_valid(tile, n_lat, n_ctx)

    @pl.when(s == 0)
    def _():
        c_s[...] = jnp.zeros_like(c_s)
        n_s[...] = jnp.zeros_like(n_s)
        m_s[...] = jnp.zeros_like(m_s)

    row = lax.broadcasted_iota(jnp.int32, (L, 1), 0)

    def conv_silu(x_ref, xp_ref, xn_ref, lanes):
        x = x_ref[...].astype(F32)
        xp = xp_ref[ML_HALO - 1:ML_HALO, :].astype(F32) * prev_ok
        xn0 = xn_ref[0:1, :].astype(F32) * next_ok
        xn1 = xn_ref[1:2, :].astype(F32) * next_ok
        x_m1 = jnp.where(row == 0, xp, pltpu.roll(x, 1, 0))
        x_p1 = jnp.where(row == L - 1, xn0, pltpu.roll(x, L - 1, 0))
        x_p2 = jnp.where(row == L - 1, xn1, jnp.where(row == L - 2, xn0, pltpu.roll(x, L - 2, 0)))
        y = (cb_ref[:, lanes] + cw_ref[0:1, lanes] * x_m1 + cw_ref[1:2, lanes] * x
             + cw_ref[2:3, lanes] * x_p1 + cw_ref[3:4, lanes] * x_p2)
        return y * jax.nn.sigmoid(y)

    if rev:
        qc, kc = qc_ref[...], kc_ref[...]
    else:
        qc = conv_silu(q_ref, qp_ref, qn_ref, slice(0, ML_WIDTH)).astype(BF16)
        kc = (conv_silu(k_ref, kp_ref, kn_ref, slice(ML_WIDTH, 2 * ML_WIDTH)) * (ML_DIM ** -0.5)).astype(BF16)
        qc_ref[...], kc_ref[...] = qc, kc
    gates = gt_ref[...] + gb_ref[...]

    ti = lax.broadcasted_iota(jnp.int32, (L, L), 0)
    si = lax.broadcasted_iota(jnp.int32, (L, L), 1)
    causal = (si >= ti) if rev else (si <= ti)
    nt = (((1,), (1,)), ((), ()))
    tn = (((0,), (0,)), ((), ()))
    dsel = 2 * ML_HEADS if rev else 0

    old_states = [(c_s[h], n_s[h], m_s[h][0:1, 0:1]) for h in range(ML_HEADS)]
    new_states = []

    for h in range(ML_HEADS):
        lanes = slice(h * ML_DIM, (h + 1) * ML_DIM)
        q, k, v = qc[:, lanes], kc[:, lanes], v_ref[:, lanes]
        ig = gates[:, dsel + h:dsel + h + 1]
        fg = gates[:, dsel + ML_HEADS + h:dsel + ML_HEADS + h + 1]
        lf = jnp.minimum(fg, 0.0) - jnp.log1p(jnp.exp(-jnp.abs(fg)))
        cmat, nvec, m_prev = old_states[h]

        lf_row = jnp.transpose(jnp.broadcast_to(lf, (L, L)))
        bcum = jnp.sum(jnp.where(causal, lf_row, 0.0), axis=-1, keepdims=True)
        e_row = jnp.transpose(jnp.broadcast_to(bcum - ig, (L, L)))
        dmat = jnp.where(causal, bcum - e_row, -jnp.inf)
        g_inter = bcum + m_prev
        m_t = jnp.maximum(g_inter, jnp.max(dmat, axis=-1, keepdims=True))
        w_inter = jnp.exp(g_inter - m_t)
        smat = lax.dot_general(q, k, nt, preferred_element_type=F32) * jnp.exp(dmat - m_t)
        num =(jnp.dot(smat.astype(BF16), v, preferred_element_type=F32)
               + w_inter * lax.dot_general(q, cmat.astype(BF16), nt, preferred_element_type=F32))
        den = (jnp.sum(smat, axis=-1, keepdims=True)
               + w_inter * jnp.sum(q.astype(F32) * nvec, axis=-1, keepdims=True))
        hout = num / jnp.maximum(jnp.abs(den), jnp.exp(-m_t))

        total = jnp.sum(lf, axis=0, keepdims=True)
        decay = total + m_prev
        w_s = total - bcum + ig
        m_new = jnp.maximum(decay, jnp.max(w_s, axis=0, keepdims=True))
        ws = jnp.exp(w_s - m_new)
        sc = jnp.exp(decay - m_new)
        wv = (ws * v.astype(F32)).astype(BF16)
        new_states.append((sc * cmat + lax.dot_general(wv, k, tn, preferred_element_type=F32),
                           sc * nvec + jnp.sum(ws * k.astype(F32), axis=0, keepdims=True),
                           jnp.broadcast_to(m_new, m_s.shape[1:])))

        if rev:
            hsum = hout + hf_ref[:, lanes]
            ms = jnp.mean(hsum * hsum, axis=-1, keepdims=True)
            hn = hsum * lax.rsqrt(ms + EPS) * ng_ref[:, lanes]
            o_ref[:, lanes] = (hn * jax.nn.sigmoid(og_ref[:, lanes].astype(F32))).astype(o_ref.dtype)
        else:
            o_ref[:, lanes] = hout

    for h, (c_new, n_new, m_new) in enumerate(new_states):
        c_s[h], n_s[h], m_s[h] = c_new, n_new, m_new


def _mlstm(ub, uf, conv_w, conv_b, gate_b, norm_g, l, d, fwd_outs, *, bsz, npb, n_lat, n_ctx):
    t = ub.shape[0]
    L = ML_CHUNK
    rev = d == 1
    tile = functools.partial(_scan_tile, n_lat=n_lat, n_ctx=n_ctx, rev=rev)
    wpb = npb // ML_WIDTH
    hpb = L // ML_HALO

    def main(off):
        return pl.BlockSpec((L, ML_WIDTH), lambda b, s: (tile(s), b * wpb + off // ML_WIDTH))

    def prev(off):
        return pl.BlockSpec((ML_HALO, ML_WIDTH),
                            lambda b, s: (jnp.maximum(tile(s) * hpb - 1, 0), b * wpb + off // ML_WIDTH))

    def nxt(off):
        return pl.BlockSpec((ML_HALO, ML_WIDTH),
                            lambda b, s: (jnp.minimum((tile(s) + 1) * hpb, t // ML_HALO - 1), b * wpb + off // ML_WIDTH))

    per_batch = pl.BlockSpec((L, ML_WIDTH), lambda b, s: (tile(s), b))
    gate_specs = [pl.BlockSpec((L, GATE_PAD), lambda b, s: (tile(s), b)),
                  pl.BlockSpec((None, 1, GATE_PAD), lambda b, s: (l, 0, 0))]
    if rev:
        qc, kc, hf = fwd_outs
        in_specs = [per_batch, per_batch, main(OFF_MV), *gate_specs, main(OFF_MO),
                    pl.BlockSpec((None, 1, ML_WIDTH), lambda b, s: (l, 0, 0)), per_batch]
        args = [qc, kc, ub, uf, gate_b, ub, norm_g, hf]
        out_specs = per_batch
        out_shape = jax.ShapeDtypeStruct((t, bsz * ML_WIDTH), BF16)
    else:
        in_specs = [main(OFF_MQ), prev(OFF_MQ), nxt(OFF_MQ), main(OFF_MK), prev(OFF_MK), nxt(OFF_MK), main(OFF_MV),
                    gate_specs[0],
                    pl.BlockSpec((None, CONV_W, 2 * ML_WIDTH), lambda b, s: (l, 0, 0)),
                    pl.BlockSpec((None, 1, 2 * ML_WIDTH), lambda b, s: (l, 0, 0)),
                    gate_specs[1]]
        args = [ub, ub, ub, ub, ub, ub, ub, uf, conv_w, conv_b, gate_b]
        out_specs = [per_batch, per_batch, per_batch]
        out_shape = [jax.ShapeDtypeStruct((t, bsz * ML_WIDTH), dt) for dt in (F32, BF16, BF16)]
    return pl.pallas_call(
        functools.partial(_mlstm_kernel, rev=rev, n_lat=n_lat, n_ctx=n_ctx),
        grid=(bsz, n_lat + n_ctx),
        in_specs=in_specs,
        out_specs=out_specs,
        out_shape=out_shape,
        scratch_shapes=[
            pltpu.VMEM((ML_HEADS, ML_DIM, ML_DIM), F32),
            pltpu.VMEM((ML_HEADS, 1, ML_DIM), F32),
            pltpu.VMEM((ML_HEADS, SUBLANE, LANE), F32),
        ],
        compiler_params=_cparams(("arbitrary", "arbitrary")),
        name="mlstm_rev" if rev else "mlstm_fwd",
    )(*args)


def _rope_tables(seq, ctx_len):
    rows = seq // GRID_W
    row = jnp.repeat(jnp.arange(rows, dtype=F32), GRID_W)
    col = jnp.tile(jnp.arange(GRID_W, dtype=F32), rows)
    quarter = DA_DIM // 4
    inv = ROPE_BASE ** (-jnp.arange(quarter, dtype=F32) / quarter)
    ar, ac = row[:, None] * inv, col[:, None] * inv
    ang = jnp.concatenate([ar, ac, ar, ac], axis=-1)
    cos, sin = jnp.cos(ang), jnp.sin(ang)
    sin_signed = jnp.where(jnp.arange(DA_DIM) < DA_DIM // 2, -sin, sin)
    pad = lambda tbl, fill: jnp.concatenate([tbl, jnp.full((ctx_len, DA_DIM), fill, F32)], axis=0)
    return pad(cos, 1.0), pad(sin_signed, 0.0)


def _rope_lane_order(x):
    quarter = DA_DIM // 4
    g = x.reshape(*x.shape[:-1], x.shape[-1] // DA_DIM, 4, quarter)
    return g[..., jnp.array([0, 2, 1, 3]), :].reshape(x.shape)


def _row_tile(t, parts, seq):
    tm = t // parts
    assert tm * parts == t and tm % ROW_CHUNK == 0 and seq % ROW_CHUNK == 0, (t, parts)
    return tm


def kernel(x, c, ctx, c_ctx, w_mod, b_mod, norm1_g, norm2_g, w_in, attn_qnorm_g, attn_knorm_g, attn_lambda,
           attn_subln_g, rnn_conv_w, rnn_conv_b, rnn_wa, rnn_ba, rnn_wx, rnn_bx, rnn_lambda, ml_conv_w, ml_conv_b,
           ml_gate_b, ml_norm_g, w_branch_attn, w_branch_rnn, w_branch_ml, w_out, w_ffn1, w_ffn3, w_ffn2):
    bsz, seq, d = x.shape
    ctx_len = ctx.shape[1]
    depth = w_in.shape[0]
    t = seq + ctx_len
    assert bsz == SUBLANE and seq % 256 == 0 and ctx_len % 256 == 0 and d % TN_PROJ == 0

    cuts = [0]
    for w in (DA_WIDTH, DA_WIDTH, DA_WIDTH, RNN_WIDTH, RNN_WIDTH, ML_WIDTH, ML_WIDTH, ML_WIDTH, ML_WIDTH, ML_GATES,
              N_BRANCH * d):
        cuts.append(cuts[-1] + w)
    seg = lambda k: w_in[:, :, cuts[k]:cuts[k + 1]]
    w_b = jnp.concatenate([_rope_lane_order(seg(0)), _rope_lane_order(seg(1)), seg(2), seg(5), seg(6), seg(7), seg(8),
                           seg(10), seg(3), seg(4)], axis=-1).astype(BF16)
    npb = w_b.shape[-1] - 2 * RNN_WIDTH
    w_mg = jnp.pad(seg(9), ((0, 0), (0, 0), (0, GATE_PAD - ML_GATES))).astype(BF16)
    assert npb % ML_WIDTH == 0
    qkg = _rope_lane_order(jnp.stack([attn_qnorm_g, attn_knorm_g], axis=1))
    rope = _rope_tables(seq, ctx_len)
    w_gate = jnp.concatenate([rnn_wa, rnn_wx], axis=-1).astype(BF16)
    rg_bias = jnp.concatenate([rnn_ba[:, :, None], rnn_bx[:, :, None], rnn_lambda[:, :, None],
                               jnp.zeros((depth, 2, SUBLANE - 3, RNN_WIDTH), F32)], axis=2)
    gate_b = jnp.pad(ml_gate_b, ((0, 0), (0, GATE_PAD - ML_GATES)))[:, None, :]

    def tile_major(w, tn):
        dep, k, n = w.shape
        return w.astype(BF16).reshape(dep, k, n // tn, tn).transpose(0, 2, 1, 3)

    w_b = tile_major(w_b, TN_PROJ)
    wba, wbr, wbm = (tile_major(w, TN_PROJ) for w in (w_branch_attn, w_branch_rnn, w_branch_ml))
    w1, w3 = tile_major(w_ffn1, TN_PROJ), tile_major(w_ffn3, TN_PROJ)
    wo, w2 = w_out.astype(BF16), w_ffn2.astype(BF16)
    row3 = lambda a: a[:, None, :]

    cvecs = jnp.concatenate([c, c_ctx[None, :], jnp.zeros((2 * SUBLANE - bsz - 1, d), F32)], axis=0)
    mod_all = _modvec(cvecs, w_mod, b_mod)[:, :bsz + 1]
    mod_all = jnp.pad(mod_all.reshape(depth, bsz + 1, N_MOD, d), ((0, 0), (0, 0), (0, MOD_ROWS - N_MOD), (0, 0)))

    h = jnp.concatenate([x, ctx], axis=1)
    tm_proj = _row_tile(t, 4, seq)
    tm_half = _row_tile(t, 8, seq)
    n_lat_rg, n_ctx_rg = seq // RG_TS, ctx_len // RG_TS
    n_lat_ml, n_ctx_ml = seq // ML_CHUNK, ctx_len // ML_CHUNK

    for l in range(depth):
        lam_init = 0.8 - 0.6 * math.exp(-0.3 * l)
        mod = mod_all[l]
        ub, ur, uf = _inproj(h, mod, row3(norm1_g), w_b, w_mg, l, qkg, rope, seq=seq, tm=tm_proj, tn=TN_PROJ)
        a = _attention(ub, attn_lambda, row3(attn_subln_g), l, bsz=bsz, seq=seq, npb=npb, tq=256, lam_init=lam_init)
        uf3 = ur.reshape(t, bsz, 2 * RNN_WIDTH).astype(F32)
        hf3 = _rglru(uf3, rnn_conv_w, row3(rnn_conv_b), w_gate, rg_bias, l, 0, None, n_lat=n_lat_rg, n_ctx=n_ctx_rg)
        r = _rglru(uf3, rnn_conv_w, row3(rnn_conv_b), w_gate, rg_bias, l, 1, hf3, n_lat=n_lat_rg, n_ctx=n_ctx_rg)
        r = r.astype(BF16).reshape(t, bsz * RNN_WIDTH)
        hf, qc, kc = _mlstm(ub, uf, ml_conv_w, row3(ml_conv_b), gate_b, row3(ml_norm_g), l, 0, None, bsz=bsz,
                            npb=npb, n_lat=n_lat_ml, n_ctx=n_ctx_ml)
        m = _mlstm(ub, uf, ml_conv_w, row3(ml_conv_b), gate_b, row3(ml_norm_g), l, 1, (qc, kc, hf), bsz=bsz,
                   npb=npb, n_lat=n_lat_ml, n_ctx=n_ctx_ml)
        last = l == depth - 1
        rows, tm = (seq, seq // 8) if last else (t, tm_half)
        h = _merge(h, mod, a, r, m, ub, wba, wbr, wbm, wo, l, seq=seq, npb=npb, rows=rows, tm=tm, tz=TN_PROJ)
        h = _ffn(h, mod, row3(norm2_g), w1, w3, w2, l, seq=seq, tm=tm)
    return h
```

```python
import functools
import math

import jax
import jax.numpy as jnp
from jax import lax
from jax.experimental import pallas as pl
from jax.experimental.pallas import tpu as pltpu

F32 = jnp.float32
BF16 = jnp.bfloat16

EPS = 1e-6
GRID_W = 64
ROPE_BASE = 10000.0
N_MOD = 6
N_BRANCH = 3
MOD_ROWS = 8

DA_HEADS = 4
DA_DIM = 128
DA_WIDTH = DA_HEADS * 2 * DA_DIM
DA_SCALE = DA_DIM ** -0.5

RNN_WIDTH = 1024
RNN_BLOCKS = 8
RNN_BLOCK = RNN_WIDTH // RNN_BLOCKS
LRU_C = 8.0
CONV_W = 4

ML_HEADS = 4
ML_DIM = 256
ML_WIDTH = ML_HEADS * ML_DIM
ML_GATES = 2 * 2 * ML_HEADS
ML_CHUNK = 128

LANE = 128
SUBLANE = 8
VMEM_LIMIT = 56 * 1024 * 1024

OFF_AQ, OFF_AK, OFF_AV = 0, DA_WIDTH, 2 * DA_WIDTH
OFF_MQ = 3 * DA_WIDTH
OFF_MK, OFF_MV, OFF_MO = OFF_MQ + ML_WIDTH, OFF_MQ + 2 * ML_WIDTH, OFF_MQ + 3 * ML_WIDTH
OFF_BG = OFF_MQ + 4 * ML_WIDTH
GATE_PAD = LANE

TN_PROJ = 512
ROW_CHUNK = 16


def _cparams(sem):
    return pltpu.CompilerParams(dimension_semantics=sem, vmem_limit_bytes=VMEM_LIMIT)


def _modvec_kernel(s_ref, w_ref, b_ref, o_ref):
    s = s_ref[...]
    s = s * jax.nn.sigmoid(s)
    o_ref[...] = jnp.dot(s.astype(BF16), w_ref[...].astype(BF16), preferred_element_type=F32) + b_ref[...]


def _modvec(cvecs, w_mod, b_mod):
    depth, d, nm = w_mod.shape
    rows = cvecs.shape[0]
    tn = 1024
    return pl.pallas_call(
        _modvec_kernel,
        grid=(depth, nm // tn),
        in_specs=[
            pl.BlockSpec((rows, d), lambda l, j: (0, 0)),
            pl.BlockSpec((None, d, tn), lambda l, j: (l, 0, j)),
            pl.BlockSpec((None, 1, tn), lambda l, j: (l, 0, j)),
        ],
        out_specs=pl.BlockSpec((None, rows, tn), lambda l, j: (l, 0, j)),
        out_shape=jax.ShapeDtypeStruct((depth, rows, nm), F32),
        compiler_params=_cparams(("arbitrary", "arbitrary")),
        name="modvec",
    )(cvecs, w_mod, b_mod.reshape(depth, 1, nm))


def _norm_modulate(h_ref, ml_ref, mc_ref, g_ref, xn_ref, *, row0, seq, shift_row):
    tm = xn_ref.shape[0]
    g = g_ref[...]
    sh_l, sc_l = ml_ref[shift_row:shift_row + 1, :], ml_ref[shift_row + 1:shift_row + 2, :]
    sh_c, sc_c = mc_ref[shift_row:shift_row + 1, :], mc_ref[shift_row + 1:shift_row + 2, :]

    def body(c, carry):
        r0 = pl.multiple_of(c * ROW_CHUNK, ROW_CHUNK)
        x = h_ref[pl.ds(r0, ROW_CHUNK), :]
        ms = jnp.mean(x * x, axis=-1, keepdims=True)
        y = x * lax.rsqrt(ms + EPS) * g
        is_ctx = (row0 + r0) >= seq
        sh = jnp.where(is_ctx, sh_c, sh_l)
        sc = jnp.where(is_ctx, sc_c, sc_l)
        xn_ref[pl.ds(r0, ROW_CHUNK), :] = (y * (1.0 + sc) + sh).astype(BF16)
        return carry

    lax.fori_loop(0, tm // ROW_CHUNK, body, 0, unroll=2)


def _gated_residual(h_ref, acc_ref, ml_ref, mc_ref, o_ref, *, row0, seq, gate_row, col0=0):
    tm, width = acc_ref.shape
    cols = pl.ds(col0, width)
    gt_l = ml_ref[gate_row:gate_row + 1, cols]
    gt_c = mc_ref[gate_row:gate_row + 1, cols]

    def body(c, carry):
        r0 = pl.multiple_of(c * ROW_CHUNK, ROW_CHUNK)
        gt = jnp.where((row0 + r0) >= seq, gt_c, gt_l)
        o_ref[pl.ds(r0, ROW_CHUNK), :] = h_ref[pl.ds(r0, ROW_CHUNK), cols] + gt * acc_ref[pl.ds(r0, ROW_CHUNK), :]
        return carry

    lax.fori_loop(0, tm // ROW_CHUNK, body, 0, unroll=2)


EPILOGUE_ROWS_CAP = 272


def _epilogue_rows(tm):
    return max(r for r in range(ROW_CHUNK, min(tm, EPILOGUE_ROWS_CAP) + 1, ROW_CHUNK) if tm % r == 0)


def _inproj_kernel(h_ref, ml_ref, mc_ref, g_ref, w_ref, wg_ref, qkg_ref, cos_ref, sn_ref, u_ref, r_ref, gt_ref,
                   xn_ref, acc_ref, *, seq, n_q_tiles, n_k_tiles, n_u_tiles):
    i, j = pl.program_id(1), pl.program_id(2)
    tm, tn = acc_ref.shape

    @pl.when(j == 0)
    def _():
        _norm_modulate(h_ref, ml_ref, mc_ref, g_ref, xn_ref, row0=i * tm, seq=seq, shift_row=0)
        gt_ref[...] = jnp.dot(xn_ref[...], wg_ref[...], preferred_element_type=F32)

    def qk_epilogue(gain, scale):
        rc = _epilogue_rows(tm)

        def body(c, carry):
            r0 = pl.multiple_of(c * rc, rc)
            cs, sn = cos_ref[pl.ds(r0, rc), :], sn_ref[pl.ds(r0, rc), :]
            for gidx in range(tn // DA_DIM):
                lanes = slice(gidx * DA_DIM, (gidx + 1) * DA_DIM)
                x = acc_ref[pl.ds(r0, rc), lanes]
                ms = jnp.mean(x * x, axis=-1, keepdims=True)
                y = x * lax.rsqrt(ms + EPS) * gain
                y = y * cs + pltpu.roll(y, DA_DIM // 2, 1) * sn
                u_ref[pl.ds(r0, rc), lanes] = (y * scale).astype(u_ref.dtype)
            return carry

        lax.fori_loop(0, tm // rc, body, 0)

    @pl.when(j < n_q_tiles)
    def _():
        acc_ref[...] = jnp.dot(xn_ref[...], w_ref[...], preferred_element_type=F32)
        qk_epilogue(qkg_ref[0:1, :], DA_SCALE * math.log2(math.e))

    @pl.when(jnp.logical_and(j >= n_q_tiles, j < n_q_tiles + n_k_tiles))
    def _():
        acc_ref[...] = jnp.dot(xn_ref[...], w_ref[...], preferred_element_type=F32)
        qk_epilogue(qkg_ref[1:2, :], 1.0)

    @pl.when(jnp.logical_and(j >= n_q_tiles + n_k_tiles, j < n_u_tiles))
    def _():
        u_ref[...] = jnp.dot(xn_ref[...], w_ref[...], preferred_element_type=F32).astype(u_ref.dtype)

    @pl.when(j >= n_u_tiles)
    def _():
        r_ref[...] = jnp.dot(xn_ref[...], w_ref[...], preferred_element_type=F32).astype(r_ref.dtype)


def _inproj(h, mod, norm_g, w, w_gate, l, qkg, rope, *, seq, tm, tn):
    bsz, t, d = h.shape
    nr = 2 * RNN_WIDTH // tn
    nu = w.shape[-1] // tn - nr
    n_q = DA_WIDTH // tn
    cos, sn = rope
    kern = functools.partial(_inproj_kernel, seq=seq, n_q_tiles=n_q, n_k_tiles=n_q, n_u_tiles=nu)
    return pl.pallas_call(
        kern,
        grid=(bsz, t // tm, nu + nr),
        in_specs=[
            pl.BlockSpec((None, tm, d), lambda b, i, j: (b, i, 0)),
            pl.BlockSpec((None, MOD_ROWS, d), lambda b, i, j: (b, 0, 0)),
            pl.BlockSpec((None, MOD_ROWS, d), lambda b, i, j: (bsz, 0, 0)),
            pl.BlockSpec((None, 1, d), lambda b, i, j: (l, 0, 0)),
            pl.BlockSpec((None, d, tn), lambda b, i, j: (l, 0, j)),
            pl.BlockSpec((None, d, GATE_PAD), lambda b, i, j: (l, 0, 0)),
            pl.BlockSpec((None, 2, DA_DIM), lambda b, i, j: (l, 0, 0)),
            pl.BlockSpec((tm, DA_DIM), lambda b, i, j: (i, 0)),
            pl.BlockSpec((tm, DA_DIM), lambda b, i, j: (i, 0)),
        ],
        out_specs=[
            pl.BlockSpec((tm, tn), lambda b, i, j: (i, b * nu + jnp.minimum(j, nu - 1))),
            pl.BlockSpec((tm, tn), lambda b, i, j: (i, b * nr + jnp.clip(j - nu, 0, nr - 1))),
            pl.BlockSpec((tm, GATE_PAD), lambda b, i, j: (i, b)),
        ],
        out_shape=[
            jax.ShapeDtypeStruct((t, bsz * nu * tn), BF16),
            jax.ShapeDtypeStruct((t, bsz * nr * tn), BF16),
            jax.ShapeDtypeStruct((t, bsz * GATE_PAD), F32),
        ],
        scratch_shapes=[pltpu.VMEM((tm, d), BF16), pltpu.VMEM((tm, tn), F32)],
        compiler_params=_cparams(("arbitrary", "arbitrary", "arbitrary")),
        name="inproj",
    )(h, mod, mod, norm_g, w, w_gate, qkg, cos, sn)


ATT_KB = 256


def _attn_kernel(lam_ref, q_ref, k_ref, v_ref, g_ref, o_ref, s0_s, s1_s, p0_s, p1_s, *, seq, lam_init, n_lat_tiles):
    qi = pl.program_id(2)
    tq = q_ref.shape[0]
    n_chunks = k_ref.shape[0] // ATT_KB
    s_bufs, p_bufs = (s0_s, s1_s), (p0_s, p1_s)
    lv = lam_ref[...]
    lam = (jnp.exp(jnp.sum(lv[0:1] * lv[1:2], axis=-1, keepdims=True))
           - jnp.exp(jnp.sum(lv[2:3] * lv[3:4], axis=-1, keepdims=True)) + lam_init)
    nt = (((1,), (1,)), ((), ()))

    def lane_groups(x):
        return [x[:, g * LANE:(g + 1) * LANE] for g in range(x.shape[1] // LANE)]

    def keys_of(c):
        return slice(c * ATT_KB, (c + 1) * ATT_KB)

    def score_chunk(mp, c, q, m_part):
        lanes = slice(mp * DA_DIM, (mp + 1) * DA_DIM)
        s = lax.dot_general(q, k_ref[keys_of(c), lanes], nt, preferred_element_type=F32)
        s_bufs[mp][:, keys_of(c)] = s
        for grp in lane_groups(s):
            m_part = jnp.maximum(m_part, grp)
        return m_part

    def exp_chunk(mp, c, m, l_part):
        p = jnp.exp2(s_bufs[mp][:, keys_of(c)] - m)
        for grp in lane_groups(p):
            l_part = l_part + grp
        p_bufs[mp][:, keys_of(c)] = p.astype(BF16)
        return l_part

    def pv_chunk(mp, c, acc):
        part = jnp.dot(p_bufs[mp][:, keys_of(c)], v_ref[keys_of(c), :], preferred_element_type=F32)
        return part if acc is None else acc + part

    def attend(c_lo, c_hi):
        chunks = range(c_lo, c_hi)
        neg_inf = jnp.full((tq, LANE), -jnp.inf, F32)
        zeros = jnp.zeros((tq, LANE), F32)
        m0_part = neg_inf
        q0 = q_ref[:, :DA_DIM]
        for c in chunks:
            m0_part = score_chunk(0, c, q0, m0_part)
        m0 = jnp.max(m0_part, axis=-1, keepdims=True)
        q1 = q_ref[:, DA_DIM:]
        m1_part, l0_part = neg_inf, zeros
        for c in chunks:
            m1_part = score_chunk(1, c, q1, m1_part)
            l0_part = exp_chunk(0, c, m0, l0_part)
        m1 = jnp.max(m1_part, axis=-1, keepdims=True)
        l0 = jnp.sum(l0_part, axis=-1, keepdims=True)
        l1_part, o0 = zeros, None
        for c in chunks:
            l1_part = exp_chunk(1, c, m1, l1_part)
            o0 = pv_chunk(0, c, o0)
        l1 = jnp.sum(l1_part, axis=-1, keepdims=True)
        span = slice(c_lo * ATT_KB, c_hi * ATT_KB)
        o1 = jnp.dot(p1_s[:, span], v_ref[span, :], preferred_element_type=F32)
        o = o0 * (1.0 / l0) - o1 * (lam / l1)
        ms = jnp.mean(o * o, axis=-1, keepdims=True)
        o_ref[...] = (o * lax.rsqrt(ms + EPS) * g_ref[...] * (1.0 - lam_init)).astype(o_ref.dtype)

    @pl.when(qi < n_lat_tiles)
    def _():
        attend(0, n_chunks)

    @pl.when(qi >= n_lat_tiles)
    def _():
        attend(seq // ATT_KB, n_chunks)


def _attention(ub, attn_lambda, subln_g, l, *, bsz, seq, npb, tq, lam_init):
    t = ub.shape[0]
    hw = 2 * DA_DIM
    cpb = npb // hw
    kern = functools.partial(_attn_kernel, seq=seq, lam_init=lam_init, n_lat_tiles=seq // tq)
    return pl.pallas_call(
        kern,
        grid=(bsz, DA_HEADS, t // tq),
        in_specs=[
            pl.BlockSpec((None, 4, DA_DIM), lambda b, h, qi: (l, 0, 0)),
            pl.BlockSpec((tq, hw), lambda b, h, qi: (qi, b * cpb + OFF_AQ // hw + h)),
            pl.BlockSpec((t, hw), lambda b, h, qi: (0, b * cpb + OFF_AK // hw + h)),
            pl.BlockSpec((t, hw), lambda b, h, qi: (0, b * cpb + OFF_AV // hw + h)),
            pl.BlockSpec((None, 1, hw), lambda b, h, qi: (l, 0, 0)),
        ],
        out_specs=pl.BlockSpec((tq, hw), lambda b, h, qi: (qi, b * DA_HEADS + h)),
        out_shape=jax.ShapeDtypeStruct((t, bsz * DA_WIDTH), BF16),
        scratch_shapes=[pltpu.VMEM((tq, t), F32), pltpu.VMEM((tq, t), F32),
                        pltpu.VMEM((tq, t), BF16), pltpu.VMEM((tq, t), BF16)],
        compiler_params=_cparams(("arbitrary", "arbitrary", "arbitrary")),
        name="diff_attention",
    )(attn_lambda, ub, ub, ub, subln_g)


def _merge_kernel(h_ref, ml_ref, mc_ref, a_ref, r_ref, m_ref, ga_ref, gr_ref, gm_ref, wa_ref, wr_ref, wm_ref, wo_ref,
                  o_ref, z_s, *, seq, nz):
    i, j = pl.program_id(1), pl.program_id(2)
    tm = z_s.shape[0]
    tz = wa_ref.shape[1]
    tn = o_ref.shape[1]

    def branch(x_ref, g_ref, w_ref):
        y = jnp.dot(x_ref[...], w_ref[...], preferred_element_type=F32)
        return jax.nn.sigmoid(g_ref[...].astype(F32)) * y

    @pl.when(j < nz)
    def _():
        z = branch(a_ref, ga_ref, wa_ref) + branch(r_ref, gr_ref, wr_ref) + branch(m_ref, gm_ref, wm_ref)
        z_s[:, pl.ds(pl.multiple_of(j * tz, tz), tz)] = z.astype(BF16)

    @pl.when(j >= nz)
    def _():
        o_ref[...] = jnp.dot(z_s[...], wo_ref[...], preferred_element_type=F32)
        _gated_residual(h_ref, o_ref, ml_ref, mc_ref, o_ref, row0=i * tm, seq=seq, gate_row=2,
                        col0=pl.multiple_of((j - nz) * tn, tn))


def _merge(h, mod, a, r, m, ub, wba, wbr, wbm, wo, l, *, seq, npb, rows, tm, tz):
    bsz, _, d = h.shape
    nz = d // tz
    gpb = npb // tz
    g0 = OFF_BG // tz

    nn = d // tz

    def gate_spec(k):
        return pl.BlockSpec((tm, tz), lambda b, i, j: (i, b * gpb + g0 + k * nz + jnp.minimum(j, nz - 1)))

    def branch_spec(width):
        return pl.BlockSpec((tm, width), lambda b, i, j: (i, b))

    def w_spec(width):
        return pl.BlockSpec((None, width, tz), lambda b, i, j: (l, 0, jnp.minimum(j, nz - 1)))

    return pl.pallas_call(
        functools.partial(_merge_kernel, seq=seq, nz=nz),
        grid=(bsz, rows // tm, nz + nn),
        in_specs=[
            pl.BlockSpec((None, tm, d), lambda b, i, j: (b, i, 0), pipeline_mode=pl.Buffered(1)),
            pl.BlockSpec((None, MOD_ROWS, d), lambda b, i, j: (b, 0, 0)),
            pl.BlockSpec((None, MOD_ROWS, d), lambda b, i, j: (bsz, 0, 0)),
            branch_spec(DA_WIDTH), branch_spec(RNN_WIDTH), branch_spec(ML_WIDTH),
            gate_spec(0), gate_spec(1), gate_spec(2),
            w_spec(DA_WIDTH), w_spec(RNN_WIDTH), w_spec(ML_WIDTH),
            pl.BlockSpec((None, d, tz), lambda b, i, j: (l, 0, jnp.clip(j - nz, 0, nn - 1))),
        ],
        out_specs=pl.BlockSpec((None, tm, tz), lambda b, i, j: (b, i, jnp.clip(j - nz, 0, nn - 1))),
        out_shape=jax.ShapeDtypeStruct((bsz, rows, d), F32),
        scratch_shapes=[pltpu.VMEM((tm, d), BF16)],
        compiler_params=_cparams(("arbitrary", "arbitrary", "arbitrary")),
        name="merge_out",
    )(h, mod, mod, a, r, m, ub, ub, ub, wba, wbr, wbm, wo)


def _ffn_kernel(h_ref, ml_ref, mc_ref, g_ref, w1_ref, w3_ref, w2_ref, o_ref, xn_ref, act_s, *, seq, nf):
    i, j = pl.program_id(1), pl.program_id(2)
    tm = xn_ref.shape[0]
    tf = w1_ref.shape[1]
    tn = o_ref.shape[1]

    @pl.when(j == 0)
    def _():
        _norm_modulate(h_ref, ml_ref, mc_ref, g_ref, xn_ref, row0=i * tm, seq=seq, shift_row=3)

    @pl.when(j < nf)
    def _():
        xn = xn_ref[...]
        u = jnp.dot(xn, w1_ref[...], preferred_element_type=F32)
        v = jnp.dot(xn, w3_ref[...], preferred_element_type=F32)
        act_s[:, pl.ds(pl.multiple_of(j * tf, tf), tf)] = (u * jax.nn.sigmoid(u) * v).astype(BF16)

    @pl.when(j >= nf)
    def _():
        o_ref[...] = jnp.dot(act_s[...], w2_ref[...], preferred_element_type=F32)
        _gated_residual(h_ref, o_ref, ml_ref, mc_ref, o_ref, row0=i * tm, seq=seq, gate_row=5,
                        col0=pl.multiple_of((j - nf) * tn, tn))


def _ffn(h, mod, norm_g, w1, w3, w2, l, *, seq, tm, tf, tn):
    bsz, rows, d = h.shape
    dff = w1.shape[-1]
    nf, nn = dff // tf, d // tn
    return pl.pallas_call(
        functools.partial(_ffn_kernel, seq=seq, nf=nf),
        grid=(bsz, rows // tm, nf + nn),
        in_specs=[
            pl.BlockSpec((None, tm, d), lambda b, i, j: (b, i, 0), pipeline_mode=pl.Buffered(1)),
            pl.BlockSpec((None, MOD_ROWS, d), lambda b, i, j: (b, 0, 0)),
            pl.BlockSpec((None, MOD_ROWS, d), lambda b, i, j: (bsz, 0, 0)),
            pl.BlockSpec((None, 1, d), lambda b, i, j: (l, 0, 0)),
            pl.BlockSpec((None, d, tf), lambda b, i, j: (l, 0, jnp.minimum(j, nf - 1))),
            pl.BlockSpec((None, d, tf), lambda b, i, j: (l, 0, jnp.minimum(j, nf - 1))),
            pl.BlockSpec((None, dff, tn), lambda b, i, j: (l, 0, jnp.clip(j - nf, 0, nn - 1))),
        ],
        out_specs=pl.BlockSpec((None, tm, tn), lambda b, i, j: (b, i, jnp.clip(j - nf, 0, nn - 1))),
        out_shape=jax.ShapeDtypeStruct(h.shape, F32),
        scratch_shapes=[pltpu.VMEM((tm, d), BF16), pltpu.VMEM((tm, dff), BF16)],
        compiler_params=_cparams(("arbitrary", "arbitrary", "arbitrary")),
        name="swiglu_ffn",
    )(h, mod, mod, norm_g, w1, w3, w2)


def _scan_tile(s, n_lat, n_ctx, rev):
    if rev:
        return jnp.where(s < n_ctx, n_lat + n_ctx - 1 - s, n_lat - 1 - (s - n_ctx))
    return jnp.where(s < n_ctx, n_lat + s, s - n_ctx)


def _halo_valid(tile, n_lat, n_ctx):
    prev_ok = jnp.logical_and(tile != 0, tile != n_lat)
    next_ok = jnp.logical_and(tile != n_lat - 1, tile != n_lat + n_ctx - 1)
    return prev_ok.astype(F32), next_ok.astype(F32)


def _softplus(x):
    return jnp.maximum(x, 0.0) + jnp.log1p(jnp.exp(-jnp.abs(x)))


RG_CB = 512
RG_TS = 128
RG_TC = 32


def _rglru_kernel(*refs, rev, n_lat, n_ctx):
    if rev:
        x_ref, xp_ref, xn_ref, cw_ref, cb_ref, w_ref, bias_ref, g_ref, hf_ref, o_ref, xw_s, a_s, b_s, h_s = refs
    else:
        x_ref, xp_ref, xn_ref, cw_ref, cb_ref, w_ref, bias_ref, o_ref, xw_s, a_s, b_s, h_s = refs
    s = pl.program_id(1)
    ts = x_ref.shape[0]
    tile = _scan_tile(s, n_lat, n_ctx, rev)
    prev_ok, next_ok = _halo_valid(tile, n_lat, n_ctx)

    @pl.when(s == 0)
    def _():
        h_s[...] = jnp.zeros_like(h_s)

    xw_s[0:1] = xp_ref[...] * prev_ok
    xw_s[1:ts + 1] = x_ref[...]
    xw_s[ts + 1:ts + 3] = xn_ref[...] * next_ok

    ba, bx = bias_ref[0:1, :], bias_ref[1:2, :]
    sp = _softplus(-bias_ref[2:3, :])
    cw = [cw_ref[k:k + 1, :] for k in range(CONV_W)]
    cb = cb_ref[...]

    def gates(c, carry):
        t0 = pl.multiple_of(c * RG_TC, RG_TC)
        conv = cb + sum(cw[k] * xw_s[pl.ds(t0 + k, RG_TC)] for k in range(CONV_W))
        x2 = conv.reshape(RG_TC * SUBLANE, RG_CB)
        for blk in range(RG_CB // RNN_BLOCK):
            lanes = slice(blk * RNN_BLOCK, (blk + 1) * RNN_BLOCK)
            xb = x2[:, lanes]
            z = jnp.dot(xb.astype(BF16), w_ref[blk], preferred_element_type=F32)
            r = jax.nn.sigmoid(z[:, :RNN_BLOCK] + ba[:, lanes])
            ig = jax.nn.sigmoid(z[:, RNN_BLOCK:] + bx[:, lanes])
            log_a = -LRU_C * r * sp[:, lanes]
            a = jnp.exp(log_a)
            bb = jnp.sqrt(1.0 - a * a) * (ig * xb)
            a_s[pl.ds(t0, RG_TC), :, lanes] = a.reshape(RG_TC, SUBLANE, RNN_BLOCK)
            b_s[pl.ds(t0, RG_TC), :, lanes] = bb.reshape(RG_TC, SUBLANE, RNN_BLOCK)
        return carry

    lax.fori_loop(0, ts // RG_TC, gates, 0)

    def step(k, h):
        tt = (ts - 1 - k) if rev else k
        h = a_s[tt] * h + b_s[tt]
        if rev:
            o_ref[tt] = (h + hf_ref[tt]) * jax.nn.gelu(g_ref[tt])
        else:
            o_ref[tt] = h
        return h

    h_s[...] = lax.fori_loop(0, ts, step, h_s[...], unroll=8)


def _rglru(uf3, conv_w, conv_b, w_gate, bias, l, d, hf3, *, n_lat, n_ctx):
    t, bsz, _ = uf3.shape
    ts = RG_TS
    rev = d == 1
    tile = functools.partial(_scan_tile, n_lat=n_lat, n_ctx=n_ctx, rev=rev)
    nblk = RG_CB // RNN_BLOCK
    in_specs = [
        pl.BlockSpec((ts, bsz, RG_CB), lambda kc, s: (tile(s), 0, kc)),
        pl.BlockSpec((1, bsz, RG_CB), lambda kc, s: (jnp.maximum(tile(s) * ts - 1, 0), 0, kc)),
        pl.BlockSpec((2, bsz, RG_CB),
                     lambda kc, s: (jnp.minimum((tile(s) + 1) * (ts // 2), t // 2 - 1), 0, kc)),
        pl.BlockSpec((None, CONV_W, RG_CB), lambda kc, s: (l, 0, kc)),
        pl.BlockSpec((None, 1, RG_CB), lambda kc, s: (l, 0, kc)),
        pl.BlockSpec((None, None, nblk, RNN_BLOCK, 2 * RNN_BLOCK), lambda kc, s: (l, d, kc, 0, 0)),
        pl.BlockSpec((None, None, SUBLANE, RG_CB), lambda kc, s: (l, d, 0, kc)),
    ]
    args = [uf3, uf3, uf3, conv_w, conv_b, w_gate, bias]
    if rev:
        in_specs += [
            pl.BlockSpec((ts, bsz, RG_CB), lambda kc, s: (tile(s), 0, RNN_WIDTH // RG_CB + kc)),
            pl.BlockSpec((ts, bsz, RG_CB), lambda kc, s: (tile(s), 0, kc)),
        ]
        args += [uf3, hf3]
    return pl.pallas_call(
        functools.partial(_rglru_kernel, rev=rev, n_lat=n_lat, n_ctx=n_ctx),
        grid=(RNN_WIDTH // RG_CB, n_lat + n_ctx),
        in_specs=in_specs,
        out_specs=pl.BlockSpec((ts, bsz, RG_CB), lambda kc, s: (tile(s), 0, kc)),
        out_shape=jax.ShapeDtypeStruct((t, bsz, RNN_WIDTH), F32),
        scratch_shapes=[
            pltpu.VMEM((ts + CONV_W - 1, bsz, RG_CB), F32),
            pltpu.VMEM((ts, bsz, RG_CB), F32),
            pltpu.VMEM((ts, bsz, RG_CB), F32),
            pltpu.VMEM((bsz, RG_CB), F32),
        ],
        compiler_params=_cparams(("arbitrary", "arbitrary")),
        name="rglru_rev" if rev else "rglru_fwd",
    )(*args)


ML_HALO = 16


def _mlstm_kernel(*refs, rev, n_lat, n_ctx):
    if rev:
        qc_ref, kc_ref, v_ref, gt_ref, gb_ref, og_ref, ng_ref, hf_ref, o_ref, c_s, n_s, m_s = refs
    else:
        (q_ref, qp_ref, qn_ref, k_ref, kp_ref, kn_ref, v_ref, gt_ref, cw_ref, cb_ref, gb_ref,
         o_ref, qc_ref, kc_ref, c_s, n_s, m_s) = refs
    s = pl.program_id(1)
    L = ML_CHUNK
    tile = _scan_tile(s, n_lat, n_ctx, rev)
    prev_ok, next_ok = _halo_valid(tile, n_lat, n_ctx)

    @pl.when(s == 0)
    def _():
        c_s[...] = jnp.zeros_like(c_s)
        n_s[...] = jnp.zeros_like(n_s)
        m_s[...] = jnp.zeros_like(m_s)

    row = lax.broadcasted_iota(jnp.int32, (L, 1), 0)

    def conv_silu(x_ref, xp_ref, xn_ref, lanes):
        x = x_ref[...].astype(F32)
        xp = xp_ref[ML_HALO - 1:ML_HALO, :].astype(F32) * prev_ok
        xn0 = xn_ref[0:1, :].astype(F32) * next_ok
        xn1 = xn_ref[1:2, :].astype(F32) * next_ok
        x_m1 = jnp.where(row == 0, xp, pltpu.roll(x, 1, 0))
        x_p1 = jnp.where(row == L - 1, xn0, pltpu.roll(x, L - 1, 0))
        x_p2 = jnp.where(row == L - 1, xn1, jnp.where(row == L - 2, xn0, pltpu.roll(x, L - 2, 0)))
        y = (cb_ref[:, lanes] + cw_ref[0:1, lanes] * x_m1 + cw_ref[1:2, lanes] * x
             + cw_ref[2:3, lanes] * x_p1 + cw_ref[3:4, lanes] * x_p2)
        return y * jax.nn.sigmoid(y)

    if rev:
        qc, kc = qc_ref[...], kc_ref[...]
    else:
        qc = conv_silu(q_ref, qp_ref, qn_ref, slice(0, ML_WIDTH)).astype(BF16)
        kc = (conv_silu(k_ref, kp_ref, kn_ref, slice(ML_WIDTH, 2 * ML_WIDTH)) * (ML_DIM ** -0.5)).astype(BF16)
        qc_ref[...], kc_ref[...] = qc, kc
    gates = gt_ref[...] + gb_ref[...]

    ti = lax.broadcasted_iota(jnp.int32, (L, L), 0)
    si = lax.broadcasted_iota(jnp.int32, (L, L), 1)
    causal = (si >= ti) if rev else (si <= ti)
    nt = (((1,), (1,)), ((), ()))
    tn = (((0,), (0,)), ((), ()))
    dsel = 2 * ML_HEADS if rev else 0

    old_states = [(c_s[h], n_s[h], m_s[h][0:1, 0:1]) for h in range(ML_HEADS)]
    new_states = []

    for h in range(ML_HEADS):
        lanes = slice(h * ML_DIM, (h + 1) * ML_DIM)
        q, k, v = qc[:, lanes], kc[:, lanes], v_ref[:, lanes]
        ig = gates[:, dsel + h:dsel + h + 1]
        fg = gates[:, dsel + ML_HEADS + h:dsel + ML_HEADS + h + 1]
        lf = jnp.minimum(fg, 0.0) - jnp.log1p(jnp.exp(-jnp.abs(fg)))
        cmat, nvec, m_prev = old_states[h]

        lf_row = jnp.transpose(jnp.broadcast_to(lf, (L, L)))
        bcum = jnp.sum(jnp.where(causal, lf_row, 0.0), axis=-1, keepdims=True)
        e_row = jnp.transpose(jnp.broadcast_to(bcum - ig, (L, L)))
        dmat = jnp.where(causal, bcum - e_row, -jnp.inf)
        g_inter = bcum + m_prev
        m_t = jnp.maximum(g_inter, jnp.max(dmat, axis=-1, keepdims=True))
        w_inter = jnp.exp(g_inter - m_t)
        smat = lax.dot_general(q, k, nt, preferred_element_type=F32) * jnp.exp(dmat - m_t)
        num =(jnp.dot(smat.astype(BF16), v, preferred_element_type=F32)
               + w_inter * lax.dot_general(q, cmat.astype(BF16), nt, preferred_element_type=F32))
        den = (jnp.sum(smat, axis=-1, keepdims=True)
               + w_inter * jnp.sum(q.astype(F32) * nvec, axis=-1, keepdims=True))
        hout = num / jnp.maximum(jnp.abs(den), jnp.exp(-m_t))

        total = jnp.sum(lf, axis=0, keepdims=True)
        decay = total + m_prev
        w_s = total - bcum + ig
        m_new = jnp.maximum(decay, jnp.max(w_s, axis=0, keepdims=True))
        ws = jnp.exp(w_s - m_new)
        sc = jnp.exp(decay - m_new)
        wv = (ws * v.astype(F32)).astype(BF16)
        new_states.append((sc * cmat + lax.dot_general(wv, k, tn, preferred_element_type=F32),
                           sc * nvec + jnp.sum(ws * k.astype(F32), axis=0, keepdims=True),
                           jnp.broadcast_to(m_new, m_s.shape[1:])))

        if rev:
            hsum = hout + hf_ref[:, lanes]
            ms = jnp.mean(hsum * hsum, axis=-1, keepdims=True)
            hn = hsum * lax.rsqrt(ms + EPS) * ng_ref[:, lanes]
            o_ref[:, lanes] = (hn * jax.nn.sigmoid(og_ref[:, lanes].astype(F32))).astype(o_ref.dtype)
        else:
            o_ref[:, lanes] = hout

    for h, (c_new, n_new, m_new) in enumerate(new_states):
        c_s[h], n_s[h], m_s[h] = c_new, n_new, m_new


def _mlstm(ub, uf, conv_w, conv_b, gate_b, norm_g, l, d, fwd_outs, *, bsz, npb, n_lat, n_ctx):
    t = ub.shape[0]
    L = ML_CHUNK
    rev = d == 1
    tile = functools.partial(_scan_tile, n_lat=n_lat, n_ctx=n_ctx, rev=rev)
    wpb = npb // ML_WIDTH
    hpb = L // ML_HALO

    def main(off):
        return pl.BlockSpec((L, ML_WIDTH), lambda b, s: (tile(s), b * wpb + off // ML_WIDTH))

    def prev(off):
        return pl.BlockSpec((ML_HALO, ML_WIDTH),
                            lambda b, s: (jnp.maximum(tile(s) * hpb - 1, 0), b * wpb + off // ML_WIDTH))

    def nxt(off):
        return pl.BlockSpec((ML_HALO, ML_WIDTH),
                            lambda b, s: (jnp.minimum((tile(s) + 1) * hpb, t // ML_HALO - 1), b * wpb + off // ML_WIDTH))

    per_batch = pl.BlockSpec((L, ML_WIDTH), lambda b, s: (tile(s), b))
    gate_specs = [pl.BlockSpec((L, GATE_PAD), lambda b, s: (tile(s), b)),
                  pl.BlockSpec((None, 1, GATE_PAD), lambda b, s: (l, 0, 0))]
    if rev:
        qc, kc, hf = fwd_outs
        in_specs = [per_batch, per_batch, main(OFF_MV), *gate_specs, main(OFF_MO),
                    pl.BlockSpec((None, 1, ML_WIDTH), lambda b, s: (l, 0, 0)), per_batch]
        args = [qc, kc, ub, uf, gate_b, ub, norm_g, hf]
        out_specs = per_batch
        out_shape = jax.ShapeDtypeStruct((t, bsz * ML_WIDTH), BF16)
    else:
        in_specs = [main(OFF_MQ), prev(OFF_MQ), nxt(OFF_MQ), main(OFF_MK), prev(OFF_MK), nxt(OFF_MK), main(OFF_MV),
                    gate_specs[0],
                    pl.BlockSpec((None, CONV_W, 2 * ML_WIDTH), lambda b, s: (l, 0, 0)),
                    pl.BlockSpec((None, 1, 2 * ML_WIDTH), lambda b, s: (l, 0, 0)),
                    gate_specs[1]]
        args = [ub, ub, ub, ub, ub, ub, ub, uf, conv_w, conv_b, gate_b]
        out_specs = [per_batch, per_batch, per_batch]
        out_shape = [jax.ShapeDtypeStruct((t, bsz * ML_WIDTH), dt) for dt in (F32, BF16, BF16)]
    return pl.pallas_call(
        functools.partial(_mlstm_kernel, rev=rev, n_lat=n_lat, n_ctx=n_ctx),
        grid=(bsz, n_lat + n_ctx),
        in_specs=in_specs,
        out_specs=out_specs,
        out_shape=out_shape,
        scratch_shapes=[
            pltpu.VMEM((ML_HEADS, ML_DIM, ML_DIM), F32),
            pltpu.VMEM((ML_HEADS, 1, ML_DIM), F32),
            pltpu.VMEM((ML_HEADS, SUBLANE, LANE), F32),
        ],
        compiler_params=_cparams(("arbitrary", "arbitrary")),
        name="mlstm_rev" if rev else "mlstm_fwd",
    )(*args)


def _rope_tables(seq, ctx_len):
    rows = seq // GRID_W
    row = jnp.repeat(jnp.arange(rows, dtype=F32), GRID_W)
    col = jnp.tile(jnp.arange(GRID_W, dtype=F32), rows)
    quarter = DA_DIM // 4
    inv = ROPE_BASE ** (-jnp.arange(quarter, dtype=F32) / quarter)
    ar, ac = row[:, None] * inv, col[:, None] * inv
    ang = jnp.concatenate([ar, ac, ar, ac], axis=-1)
    cos, sin = jnp.cos(ang), jnp.sin(ang)
    sin_signed = jnp.where(jnp.arange(DA_DIM) < DA_DIM // 2, -sin, sin)
    pad = lambda tbl, fill: jnp.concatenate([tbl, jnp.full((ctx_len, DA_DIM), fill, F32)], axis=0)
    return pad(cos, 1.0), pad(sin_signed, 0.0)


def _rope_lane_order(x):
    quarter = DA_DIM // 4
    g = x.reshape(*x.shape[:-1], x.shape[-1] // DA_DIM, 4, quarter)
    return g[..., jnp.array([0, 2, 1, 3]), :].reshape(x.shape)


def _row_tile(t, parts, seq):
    tm = t // parts
    assert tm * parts == t and tm % ROW_CHUNK == 0 and seq % ROW_CHUNK == 0, (t, parts)
    return tm


def kernel(x, c, ctx, c_ctx, w_mod, b_mod, norm1_g, norm2_g, w_in, attn_qnorm_g, attn_knorm_g, attn_lambda,
           attn_subln_g, rnn_conv_w, rnn_conv_b, rnn_wa, rnn_ba, rnn_wx, rnn_bx, rnn_lambda, ml_conv_w, ml_conv_b,
           ml_gate_b, ml_norm_g, w_branch_attn, w_branch_rnn, w_branch_ml, w_out, w_ffn1, w_ffn3, w_ffn2):
    bsz, seq, d = x.shape
    ctx_len = ctx.shape[1]
    depth = w_in.shape[0]
    t = seq + ctx_len
    assert bsz == SUBLANE and seq % 256 == 0 and ctx_len % 256 == 0 and d % TN_PROJ == 0

    cuts = [0]
    for w in (DA_WIDTH, DA_WIDTH, DA_WIDTH, RNN_WIDTH, RNN_WIDTH, ML_WIDTH, ML_WIDTH, ML_WIDTH, ML_WIDTH, ML_GATES,
              N_BRANCH * d):
        cuts.append(cuts[-1] + w)
    seg = lambda k: w_in[:, :, cuts[k]:cuts[k + 1]]
    w_b = jnp.concatenate([_rope_lane_order(seg(0)), _rope_lane_order(seg(1)), seg(2), seg(5), seg(6), seg(7), seg(8),
                           seg(10), seg(3), seg(4)], axis=-1).astype(BF16)
    npb = w_b.shape[-1] - 2 * RNN_WIDTH
    w_mg = jnp.pad(seg(9), ((0, 0), (0, 0), (0, GATE_PAD - ML_GATES))).astype(BF16)
    assert npb % ML_WIDTH == 0
    qkg = _rope_lane_order(jnp.stack([attn_qnorm_g, attn_knorm_g], axis=1))
    rope = _rope_tables(seq, ctx_len)
    w_gate = jnp.concatenate([rnn_wa, rnn_wx], axis=-1).astype(BF16)
    rg_bias = jnp.concatenate([rnn_ba[:, :, None], rnn_bx[:, :, None], rnn_lambda[:, :, None],
                               jnp.zeros((depth, 2, SUBLANE - 3, RNN_WIDTH), F32)], axis=2)
    gate_b = jnp.pad(ml_gate_b, ((0, 0), (0, GATE_PAD - ML_GATES)))[:, None, :]
    wba, wbr, wbm, wo = (w.astype(BF16) for w in (w_branch_attn, w_branch_rnn, w_branch_ml, w_out))
    w1, w3, w2 = (w.astype(BF16) for w in (w_ffn1, w_ffn3, w_ffn2))
    row3 = lambda a: a[:, None, :]

    cvecs = jnp.concatenate([c, c_ctx[None, :], jnp.zeros((2 * SUBLANE - bsz - 1, d), F32)], axis=0)
    mod_all = _modvec(cvecs, w_mod, b_mod)[:, :bsz + 1]
    mod_all = jnp.pad(mod_all.reshape(depth, bsz + 1, N_MOD, d), ((0, 0), (0, 0), (0, MOD_ROWS - N_MOD), (0, 0)))

    h = jnp.concatenate([x, ctx], axis=1)
    tm_proj = _row_tile(t, 4, seq)
    n_lat_rg, n_ctx_rg = seq // RG_TS, ctx_len // RG_TS
    n_lat_ml, n_ctx_ml = seq // ML_CHUNK, ctx_len // ML_CHUNK

    for l in range(depth):
        lam_init = 0.8 - 0.6 * math.exp(-0.3 * l)
        mod = mod_all[l]
        ub, ur, uf = _inproj(h, mod, row3(norm1_g), w_b, w_mg, l, qkg, rope, seq=seq, tm=tm_proj, tn=TN_PROJ)
        a = _attention(ub, attn_lambda, row3(attn_subln_g), l, bsz=bsz, seq=seq, npb=npb, tq=256, lam_init=lam_init)
        uf3 = ur.reshape(t, bsz, 2 * RNN_WIDTH).astype(F32)
        hf3 = _rglru(uf3, rnn_conv_w, row3(rnn_conv_b), w_gate, rg_bias, l, 0, None, n_lat=n_lat_rg, n_ctx=n_ctx_rg)
        r = _rglru(uf3, rnn_conv_w, row3(rnn_conv_b), w_gate, rg_bias, l, 1, hf3, n_lat=n_lat_rg, n_ctx=n_ctx_rg)
        r = r.astype(BF16).reshape(t, bsz * RNN_WIDTH)
        hf, qc, kc = _mlstm(ub, uf, ml_conv_w, row3(ml_conv_b), gate_b, row3(ml_norm_g), l, 0, None, bsz=bsz,
                            npb=npb, n_lat=n_lat_ml, n_ctx=n_ctx_ml)
        m = _mlstm(ub, uf, ml_conv_w, row3(ml_conv_b), gate_b, row3(ml_norm_g), l, 1, (qc, kc, hf), bsz=bsz,
                   npb=npb, n_lat=n_lat_ml, n_ctx=n_ctx_ml)
        last = l == depth - 1
        rows, tm = (seq, seq // 4) if last else (t, tm_proj)
        h = _merge(h, mod, a, r, m, ub, wba, wbr, wbm, wo, l, seq=seq, npb=npb, rows=rows, tm=tm, tz=TN_PROJ)
        h = _ffn(h, mod, row3(norm2_g), w1, w3, w2, l, seq=seq, tm=tm, tf=TN_PROJ, tn=TN_PROJ)
    return h
```

```python
import functools
import math

import jax
import jax.numpy as jnp
from jax import lax
from jax.experimental import pallas as pl
from jax.experimental.pallas import tpu as pltpu

F32 = jnp.float32
BF16 = jnp.bfloat16

EPS = 1e-6
GRID_W = 64
ROPE_BASE = 10000.0
N_MOD = 6
N_BRANCH = 3
MOD_ROWS = 8

DA_HEADS = 4
DA_DIM = 128
DA_WIDTH = DA_HEADS * 2 * DA_DIM
DA_SCALE = DA_DIM ** -0.5

RNN_WIDTH = 1024
RNN_BLOCKS = 8
RNN_BLOCK = RNN_WIDTH // RNN_BLOCKS
LRU_C = 8.0
CONV_W = 4

ML_HEADS = 4
ML_DIM = 256
ML_WIDTH = ML_HEADS * ML_DIM
ML_GATES = 2 * 2 * ML_HEADS
ML_CHUNK = 128

LANE = 128
SUBLANE = 8
VMEM_LIMIT = 56 * 1024 * 1024

OFF_AQ, OFF_AK, OFF_AV = 0, DA_WIDTH, 2 * DA_WIDTH
OFF_MQ = 3 * DA_WIDTH
OFF_MK, OFF_MV, OFF_MO = OFF_MQ + ML_WIDTH, OFF_MQ + 2 * ML_WIDTH, OFF_MQ + 3 * ML_WIDTH
OFF_BG = OFF_MQ + 4 * ML_WIDTH
GATE_PAD = LANE

TN_PROJ = 512
ROW_CHUNK = 16


def _cparams(sem):
    return pltpu.CompilerParams(dimension_semantics=sem, vmem_limit_bytes=VMEM_LIMIT)


def _modvec_kernel(s_ref, w_ref, b_ref, o_ref):
    s = s_ref[...]
    s = s * jax.nn.sigmoid(s)
    o_ref[...] = jnp.dot(s.astype(BF16), w_ref[...].astype(BF16), preferred_element_type=F32) + b_ref[...]


def _modvec(cvecs, w_mod, b_mod):
    depth, d, nm = w_mod.shape
    rows = cvecs.shape[0]
    tn = 2048
    return pl.pallas_call(
        _modvec_kernel,
        grid=(depth, nm // tn),
        in_specs=[
            pl.BlockSpec((rows, d), lambda l, j: (0, 0)),
            pl.BlockSpec((None, d, tn), lambda l, j: (l, 0, j)),
            pl.BlockSpec((None, 1, tn), lambda l, j: (l, 0, j)),
        ],
        out_specs=pl.BlockSpec((None, rows, tn), lambda l, j: (l, 0, j)),
        out_shape=jax.ShapeDtypeStruct((depth, rows, nm), F32),
        compiler_params=_cparams(("arbitrary", "arbitrary")),
        name="modvec",
    )(cvecs, w_mod, b_mod.reshape(depth, 1, nm))


def _norm_modulate(h_ref, ml_ref, mc_ref, g_ref, xn_ref, *, row0, seq, shift_row):
    tm = xn_ref.shape[0]
    g = g_ref[...]
    sh_l, sc_l = ml_ref[shift_row:shift_row + 1, :], ml_ref[shift_row + 1:shift_row + 2, :]
    sh_c, sc_c = mc_ref[shift_row:shift_row + 1, :], mc_ref[shift_row + 1:shift_row + 2, :]

    def body(c, carry):
        r0 = pl.multiple_of(c * ROW_CHUNK, ROW_CHUNK)
        x = h_ref[pl.ds(r0, ROW_CHUNK), :]
        ms = jnp.mean(x * x, axis=-1, keepdims=True)
        y = x * lax.rsqrt(ms + EPS) * g
        is_ctx = (row0 + r0) >= seq
        sh = jnp.where(is_ctx, sh_c, sh_l)
        sc = jnp.where(is_ctx, sc_c, sc_l)
        xn_ref[pl.ds(r0, ROW_CHUNK), :] = (y * (1.0 + sc) + sh).astype(BF16)
        return carry

    lax.fori_loop(0, tm // ROW_CHUNK, body, 0, unroll=2)


def _gated_residual(h_ref, acc_ref, ml_ref, mc_ref, o_ref, *, row0, seq, gate_row, col0=0):
    tm, width = acc_ref.shape
    cols = pl.ds(col0, width)
    gt_l = ml_ref[gate_row:gate_row + 1, cols]
    gt_c = mc_ref[gate_row:gate_row + 1, cols]

    def body(c, carry):
        r0 = pl.multiple_of(c * ROW_CHUNK, ROW_CHUNK)
        gt = jnp.where((row0 + r0) >= seq, gt_c, gt_l)
        o_ref[pl.ds(r0, ROW_CHUNK), :] = h_ref[pl.ds(r0, ROW_CHUNK), cols] + gt * acc_ref[pl.ds(r0, ROW_CHUNK), :]
        return carry

    lax.fori_loop(0, tm // ROW_CHUNK, body, 0, unroll=2)


EPILOGUE_ROWS_CAP = 272


def _epilogue_rows(tm):
    return max(r for r in range(ROW_CHUNK, min(tm, EPILOGUE_ROWS_CAP) + 1, ROW_CHUNK) if tm % r == 0)


def _inproj_kernel(h_ref, ml_ref, mc_ref, g_ref, w_ref, wg_ref, qkg_ref, cos_ref, sn_ref, u_ref, r_ref, gt_ref,
                   xn_ref, acc_ref, *, seq, n_q_tiles, n_k_tiles, n_u_tiles):
    i, j = pl.program_id(1), pl.program_id(2)
    tm, tn = acc_ref.shape

    @pl.when(j == 0)
    def _():
        _norm_modulate(h_ref, ml_ref, mc_ref, g_ref, xn_ref, row0=i * tm, seq=seq, shift_row=0)
        gt_ref[...] = jnp.dot(xn_ref[...], wg_ref[...], preferred_element_type=F32)

    def qk_epilogue(gain, scale):
        rc = _epilogue_rows(tm)

        def body(c, carry):
            r0 = pl.multiple_of(c * rc, rc)
            cs, sn = cos_ref[pl.ds(r0, rc), :], sn_ref[pl.ds(r0, rc), :]
            for gidx in range(tn // DA_DIM):
                lanes = slice(gidx * DA_DIM, (gidx + 1) * DA_DIM)
                x = acc_ref[pl.ds(r0, rc), lanes]
                ms = jnp.mean(x * x, axis=-1, keepdims=True)
                y = x * lax.rsqrt(ms + EPS) * gain
                y = y * cs + pltpu.roll(y, DA_DIM // 2, 1) * sn
                u_ref[pl.ds(r0, rc), lanes] = (y * scale).astype(u_ref.dtype)
            return carry

        lax.fori_loop(0, tm // rc, body, 0)

    @pl.when(j < n_q_tiles)
    def _():
        acc_ref[...] = jnp.dot(xn_ref[...], w_ref[...], preferred_element_type=F32)
        qk_epilogue(qkg_ref[0:1, :], DA_SCALE * math.log2(math.e))

    @pl.when(jnp.logical_and(j >= n_q_tiles, j < n_q_tiles + n_k_tiles))
    def _():
        acc_ref[...] = jnp.dot(xn_ref[...], w_ref[...], preferred_element_type=F32)
        qk_epilogue(qkg_ref[1:2, :], 1.0)

    @pl.when(jnp.logical_and(j >= n_q_tiles + n_k_tiles, j < n_u_tiles))
    def _():
        u_ref[...] = jnp.dot(xn_ref[...], w_ref[...], preferred_element_type=F32).astype(u_ref.dtype)

    @pl.when(j >= n_u_tiles)
    def _():
        r_ref[...] = jnp.dot(xn_ref[...], w_ref[...], preferred_element_type=F32).astype(r_ref.dtype)


def _inproj(h, mod, norm_g, w, w_gate, l, qkg, rope, *, seq, tm, tn):
    bsz, t, d = h.shape
    nr = 2 * RNN_WIDTH // tn
    nu = w.shape[-1] // tn - nr
    n_q = DA_WIDTH // tn
    cos, sn = rope
    kern = functools.partial(_inproj_kernel, seq=seq, n_q_tiles=n_q, n_k_tiles=n_q, n_u_tiles=nu)
    return pl.pallas_call(
        kern,
        grid=(bsz, t // tm, nu + nr),
        in_specs=[
            pl.BlockSpec((None, tm, d), lambda b, i, j: (b, i, 0)),
            pl.BlockSpec((None, MOD_ROWS, d), lambda b, i, j: (b, 0, 0)),
            pl.BlockSpec((None, MOD_ROWS, d), lambda b, i, j: (bsz, 0, 0)),
            pl.BlockSpec((None, 1, d), lambda b, i, j: (l, 0, 0)),
            pl.BlockSpec((None, d, tn), lambda b, i, j: (l, 0, j)),
            pl.BlockSpec((None, d, GATE_PAD), lambda b, i, j: (l, 0, 0)),
            pl.BlockSpec((None, 2, DA_DIM), lambda b, i, j: (l, 0, 0)),
            pl.BlockSpec((tm, DA_DIM), lambda b, i, j: (i, 0)),
            pl.BlockSpec((tm, DA_DIM), lambda b, i, j: (i, 0)),
        ],
        out_specs=[
            pl.BlockSpec((tm, tn), lambda b, i, j: (i, b * nu + jnp.minimum(j, nu - 1))),
            pl.BlockSpec((tm, tn), lambda b, i, j: (i, b * nr + jnp.clip(j - nu, 0, nr - 1))),
            pl.BlockSpec((tm, GATE_PAD), lambda b, i, j: (i, b)),
        ],
        out_shape=[
            jax.ShapeDtypeStruct((t, bsz * nu * tn), BF16),
            jax.ShapeDtypeStruct((t, bsz * nr * tn), BF16),
            jax.ShapeDtypeStruct((t, bsz * GATE_PAD), F32),
        ],
        scratch_shapes=[pltpu.VMEM((tm, d), BF16), pltpu.VMEM((tm, tn), F32)],
        compiler_params=_cparams(("arbitrary", "arbitrary", "arbitrary")),
        name="inproj",
    )(h, mod, mod, norm_g, w, w_gate, qkg, cos, sn)


ATT_KB = 256


def _attn_kernel(lam_ref, q_ref, k_ref, v_ref, g_ref, o_ref, s0_s, s1_s, p0_s, p1_s, *, seq, lam_init, n_lat_tiles):
    qi = pl.program_id(2)
    tq = q_ref.shape[0]
    n_chunks = k_ref.shape[0] // ATT_KB
    s_bufs, p_bufs = (s0_s, s1_s), (p0_s, p1_s)
    lv = lam_ref[...]
    lam = (jnp.exp(jnp.sum(lv[0:1] * lv[1:2], axis=-1, keepdims=True))
           - jnp.exp(jnp.sum(lv[2:3] * lv[3:4], axis=-1, keepdims=True)) + lam_init)
    nt = (((1,), (1,)), ((), ()))

    def lane_groups(x):
        return [x[:, g * LANE:(g + 1) * LANE] for g in range(x.shape[1] // LANE)]

    def keys_of(c):
        return slice(c * ATT_KB, (c + 1) * ATT_KB)

    def score_chunk(mp, c, q, m_part):
        lanes = slice(mp * DA_DIM, (mp + 1) * DA_DIM)
        s = lax.dot_general(q, k_ref[keys_of(c), lanes], nt, preferred_element_type=F32)
        s_bufs[mp][:, keys_of(c)] = s
        for grp in lane_groups(s):
            m_part = jnp.maximum(m_part, grp)
        return m_part

    def exp_chunk(mp, c, m, l_part):
        p = jnp.exp2(s_bufs[mp][:, keys_of(c)] - m)
        for grp in lane_groups(p):
            l_part = l_part + grp
        p_bufs[mp][:, keys_of(c)] = p.astype(BF16)
        return l_part

    def pv_chunk(mp, c, acc):
        part = jnp.dot(p_bufs[mp][:, keys_of(c)], v_ref[keys_of(c), :], preferred_element_type=F32)
        return part if acc is None else acc + part

    def attend(c_lo, c_hi):
        chunks = range(c_lo, c_hi)
        neg_inf = jnp.full((tq, LANE), -jnp.inf, F32)
        zeros = jnp.zeros((tq, LANE), F32)
        m0_part = neg_inf
        q0 = q_ref[:, :DA_DIM]
        for c in chunks:
            m0_part = score_chunk(0, c, q0, m0_part)
        m0 = jnp.max(m0_part, axis=-1, keepdims=True)
        q1 = q_ref[:, DA_DIM:]
        m1_part, l0_part = neg_inf, zeros
        for c in chunks:
            m1_part = score_chunk(1, c, q1, m1_part)
            l0_part = exp_chunk(0, c, m0, l0_part)
        m1 = jnp.max(m1_part, axis=-1, keepdims=True)
        l0 = jnp.sum(l0_part, axis=-1, keepdims=True)
        l1_part, o0 = zeros, None
        for c in chunks:
            l1_part = exp_chunk(1, c, m1, l1_part)
            o0 = pv_chunk(0, c, o0)
        l1 = jnp.sum(l1_part, axis=-1, keepdims=True)
        span = slice(c_lo * ATT_KB, c_hi * ATT_KB)
        o1 = jnp.dot(p1_s[:, span], v_ref[span, :], preferred_element_type=F32)
        o = o0 * (1.0 / l0) - o1 * (lam / l1)
        ms = jnp.mean(o * o, axis=-1, keepdims=True)
        o_ref[...] = (o * lax.rsqrt(ms + EPS) * g_ref[...] * (1.0 - lam_init)).astype(o_ref.dtype)

    @pl.when(qi < n_lat_tiles)
    def _():
        attend(0, n_chunks)

    @pl.when(qi >= n_lat_tiles)
    def _():
        attend(seq // ATT_KB, n_chunks)


def _attention(ub, attn_lambda, subln_g, l, *, bsz, seq, npb, tq, lam_init):
    t = ub.shape[0]
    hw = 2 * DA_DIM
    cpb = npb // hw
    kern = functools.partial(_attn_kernel, seq=seq, lam_init=lam_init, n_lat_tiles=seq // tq)
    return pl.pallas_call(
        kern,
        grid=(bsz, DA_HEADS, t // tq),
        in_specs=[
            pl.BlockSpec((None, 4, DA_DIM), lambda b, h, qi: (l, 0, 0)),
            pl.BlockSpec((tq, hw), lambda b, h, qi: (qi, b * cpb + OFF_AQ // hw + h)),
            pl.BlockSpec((t, hw), lambda b, h, qi: (0, b * cpb + OFF_AK // hw + h)),
            pl.BlockSpec((t, hw), lambda b, h, qi: (0, b * cpb + OFF_AV // hw + h)),
            pl.BlockSpec((None, 1, hw), lambda b, h, qi: (l, 0, 0)),
        ],
        out_specs=pl.BlockSpec((tq, hw), lambda b, h, qi: (qi, b * DA_HEADS + h)),
        out_shape=jax.ShapeDtypeStruct((t, bsz * DA_WIDTH), BF16),
        scratch_shapes=[pltpu.VMEM((tq, t), F32), pltpu.VMEM((tq, t), F32),
                        pltpu.VMEM((tq, t), BF16), pltpu.VMEM((tq, t), BF16)],
        compiler_params=_cparams(("arbitrary", "arbitrary", "arbitrary")),
        name="diff_attention",
    )(attn_lambda, ub, ub, ub, subln_g)


def _merge_kernel(h_ref, ml_ref, mc_ref, a_ref, r_ref, m_ref, ga_ref, gr_ref, gm_ref, wa_ref, wr_ref, wm_ref, wo_ref,
                  o_ref, acc_ref, *, seq):
    i, j = pl.program_id(1), pl.program_id(2)
    tm = acc_ref.shape[0]

    def branch(x_ref, g_ref, w_ref):
        y = jnp.dot(x_ref[...], w_ref[...], preferred_element_type=F32)
        return jax.nn.sigmoid(g_ref[...].astype(F32)) * y

    z = branch(a_ref, ga_ref, wa_ref) + branch(r_ref, gr_ref, wr_ref) + branch(m_ref, gm_ref, wm_ref)
    part = jnp.dot(z.astype(BF16), wo_ref[...], preferred_element_type=F32)

    @pl.when(j == 0)
    def _():
        acc_ref[...] = part

    @pl.when(j > 0)
    def _():
        acc_ref[...] += part

    @pl.when(j == pl.num_programs(2) - 1)
    def _():
        _gated_residual(h_ref, acc_ref, ml_ref, mc_ref, o_ref, row0=i * tm, seq=seq, gate_row=2)


def _merge(h, mod, a, r, m, ub, wba, wbr, wbm, wo, l, *, seq, npb, rows, tm, tz):
    bsz, _, d = h.shape
    nz = d // tz
    gpb = npb // tz
    g0 = OFF_BG // tz

    def gate_spec(k):
        return pl.BlockSpec((tm, tz), lambda b, i, j: (i, b * gpb + g0 + k * nz + j))

    def branch_spec(width):
        return pl.BlockSpec((tm, width), lambda b, i, j: (i, b))

    def w_spec(width):
        return pl.BlockSpec((None, width, tz), lambda b, i, j: (l, 0, j))

    return pl.pallas_call(
        functools.partial(_merge_kernel, seq=seq),
        grid=(bsz, rows // tm, nz),
        in_specs=[
            pl.BlockSpec((None, tm, d), lambda b, i, j: (b, i, 0)),
            pl.BlockSpec((None, MOD_ROWS, d), lambda b, i, j: (b, 0, 0)),
            pl.BlockSpec((None, MOD_ROWS, d), lambda b, i, j: (bsz, 0, 0)),
            branch_spec(DA_WIDTH), branch_spec(RNN_WIDTH), branch_spec(ML_WIDTH),
            gate_spec(0), gate_spec(1), gate_spec(2),
            w_spec(DA_WIDTH), w_spec(RNN_WIDTH), w_spec(ML_WIDTH),
            pl.BlockSpec((None, tz, d), lambda b, i, j: (l, j, 0)),
        ],
        out_specs=pl.BlockSpec((None, tm, d), lambda b, i, j: (b, i, 0)),
        out_shape=jax.ShapeDtypeStruct((bsz, rows, d), F32),
        scratch_shapes=[pltpu.VMEM((tm, d), F32)],
        compiler_params=_cparams(("arbitrary", "arbitrary", "arbitrary")),
        name="merge_out",
    )(h, mod, mod, a, r, m, ub, ub, ub, wba, wbr, wbm, wo)


def _ffn_kernel(h_ref, ml_ref, mc_ref, g_ref, w1_ref, w3_ref, w2_ref, o_ref, xn_ref, act_s, *, seq, nf):
    i, j = pl.program_id(1), pl.program_id(2)
    tm = xn_ref.shape[0]
    tf = w1_ref.shape[1]
    tn = o_ref.shape[1]

    @pl.when(j == 0)
    def _():
        _norm_modulate(h_ref, ml_ref, mc_ref, g_ref, xn_ref, row0=i * tm, seq=seq, shift_row=3)

    @pl.when(j < nf)
    def _():
        xn = xn_ref[...]
        u = jnp.dot(xn, w1_ref[...], preferred_element_type=F32)
        v = jnp.dot(xn, w3_ref[...], preferred_element_type=F32)
        act_s[:, pl.ds(pl.multiple_of(j * tf, tf), tf)] = (u * jax.nn.sigmoid(u) * v).astype(BF16)

    @pl.when(j >= nf)
    def _():
        o_ref[...] = jnp.dot(act_s[...], w2_ref[...], preferred_element_type=F32)
        _gated_residual(h_ref, o_ref, ml_ref, mc_ref, o_ref, row0=i * tm, seq=seq, gate_row=5,
                        col0=pl.multiple_of((j - nf) * tn, tn))


def _ffn(h, mod, norm_g, w1, w3, w2, l, *, seq, tm, tf, tn):
    bsz, rows, d = h.shape
    dff = w1.shape[-1]
    nf, nn = dff // tf, d // tn
    return pl.pallas_call(
        functools.partial(_ffn_kernel, seq=seq, nf=nf),
        grid=(bsz, rows // tm, nf + nn),
        in_specs=[
            pl.BlockSpec((None, tm, d), lambda b, i, j: (b, i, 0), pipeline_mode=pl.Buffered(1)),
            pl.BlockSpec((None, MOD_ROWS, d), lambda b, i, j: (b, 0, 0)),
            pl.BlockSpec((None, MOD_ROWS, d), lambda b, i, j: (bsz, 0, 0)),
            pl.BlockSpec((None, 1, d), lambda b, i, j: (l, 0, 0)),
            pl.BlockSpec((None, d, tf), lambda b, i, j: (l, 0, jnp.minimum(j, nf - 1))),
            pl.BlockSpec((None, d, tf), lambda b, i, j: (l, 0, jnp.minimum(j, nf - 1))),
            pl.BlockSpec((None, dff, tn), lambda b, i, j: (l, 0, jnp.clip(j - nf, 0, nn - 1))),
        ],
        out_specs=pl.BlockSpec((None, tm, tn), lambda b, i, j: (b, i, jnp.clip(j - nf, 0, nn - 1))),
        out_shape=jax.ShapeDtypeStruct(h.shape, F32),
        scratch_shapes=[pltpu.VMEM((tm, d), BF16), pltpu.VMEM((tm, dff), BF16)],
        compiler_params=_cparams(("arbitrary", "arbitrary", "arbitrary")),
        name="swiglu_ffn",
    )(h, mod, mod, norm_g, w1, w3, w2)


def _scan_tile(s, n_lat, n_ctx, rev):
    if rev:
        return jnp.where(s < n_ctx, n_lat + n_ctx - 1 - s, n_lat - 1 - (s - n_ctx))
    return jnp.where(s < n_ctx, n_lat + s, s - n_ctx)


def _halo_valid(tile, n_lat, n_ctx):
    prev_ok = jnp.logical_and(tile != 0, tile != n_lat)
    next_ok = jnp.logical_and(tile != n_lat - 1, tile != n_lat + n_ctx - 1)
    return prev_ok.astype(F32), next_ok.astype(F32)


def _softplus(x):
    return jnp.maximum(x, 0.0) + jnp.log1p(jnp.exp(-jnp.abs(x)))


RG_CB = 512
RG_TS = 128
RG_TC = 32


def _rglru_kernel(*refs, rev, n_lat, n_ctx):
    if rev:
        x_ref, xp_ref, xn_ref, cw_ref, cb_ref, w_ref, bias_ref, g_ref, hf_ref, o_ref, xw_s, a_s, b_s, h_s = refs
    else:
        x_ref, xp_ref, xn_ref, cw_ref, cb_ref, w_ref, bias_ref, o_ref, xw_s, a_s, b_s, h_s = refs
    s = pl.program_id(1)
    ts = x_ref.shape[0]
    tile = _scan_tile(s, n_lat, n_ctx, rev)
    prev_ok, next_ok = _halo_valid(tile, n_lat, n_ctx)

    @pl.when(s == 0)
    def _():
        h_s[...] = jnp.zeros_like(h_s)

    xw_s[0:1] = xp_ref[...] * prev_ok
    xw_s[1:ts + 1] = x_ref[...]
    xw_s[ts + 1:ts + 3] = xn_ref[...] * next_ok

    ba, bx = bias_ref[0:1, :], bias_ref[1:2, :]
    sp = _softplus(-bias_ref[2:3, :])
    cw = [cw_ref[k:k + 1, :] for k in range(CONV_W)]
    cb = cb_ref[...]

    def gates(c, carry):
        t0 = pl.multiple_of(c * RG_TC, RG_TC)
        conv = cb + sum(cw[k] * xw_s[pl.ds(t0 + k, RG_TC)] for k in range(CONV_W))
        x2 = conv.reshape(RG_TC * SUBLANE, RG_CB)
        for blk in range(RG_CB // RNN_BLOCK):
            lanes = slice(blk * RNN_BLOCK, (blk + 1) * RNN_BLOCK)
            xb = x2[:, lanes]
            z = jnp.dot(xb.astype(BF16), w_ref[blk], preferred_element_type=F32)
            r = jax.nn.sigmoid(z[:, :RNN_BLOCK] + ba[:, lanes])
            ig = jax.nn.sigmoid(z[:, RNN_BLOCK:] + bx[:, lanes])
            log_a = -LRU_C * r * sp[:, lanes]
            a = jnp.exp(log_a)
            bb = jnp.sqrt(1.0 - a * a) * (ig * xb)
            a_s[pl.ds(t0, RG_TC), :, lanes] = a.reshape(RG_TC, SUBLANE, RNN_BLOCK)
            b_s[pl.ds(t0, RG_TC), :, lanes] = bb.reshape(RG_TC, SUBLANE, RNN_BLOCK)
        return carry

    lax.fori_loop(0, ts // RG_TC, gates, 0)

    def step(k, h):
        tt = (ts - 1 - k) if rev else k
        h = a_s[tt] * h + b_s[tt]
        if rev:
            o_ref[tt] = (h + hf_ref[tt]) * jax.nn.gelu(g_ref[tt])
        else:
            o_ref[tt] = h
        return h

    h_s[...] = lax.fori_loop(0, ts, step, h_s[...], unroll=8)


def _rglru(uf3, conv_w, conv_b, w_gate, bias, l, d, hf3, *, n_lat, n_ctx):
    t, bsz, _ = uf3.shape
    ts = RG_TS
    rev = d == 1
    tile = functools.partial(_scan_tile, n_lat=n_lat, n_ctx=n_ctx, rev=rev)
    nblk = RG_CB // RNN_BLOCK
    in_specs = [
        pl.BlockSpec((ts, bsz, RG_CB), lambda kc, s: (tile(s), 0, kc)),
        pl.BlockSpec((1, bsz, RG_CB), lambda kc, s: (jnp.maximum(tile(s) * ts - 1, 0), 0, kc)),
        pl.BlockSpec((2, bsz, RG_CB),
                     lambda kc, s: (jnp.minimum((tile(s) + 1) * (ts // 2), t // 2 - 1), 0, kc)),
        pl.BlockSpec((None, CONV_W, RG_CB), lambda kc, s: (l, 0, kc)),
        pl.BlockSpec((None, 1, RG_CB), lambda kc, s: (l, 0, kc)),
        pl.BlockSpec((None, None, nblk, RNN_BLOCK, 2 * RNN_BLOCK), lambda kc, s: (l, d, kc, 0, 0)),
        pl.BlockSpec((None, None, SUBLANE, RG_CB), lambda kc, s: (l, d, 0, kc)),
    ]
    args = [uf3, uf3, uf3, conv_w, conv_b, w_gate, bias]
    if rev:
        in_specs += [
            pl.BlockSpec((ts, bsz, RG_CB), lambda kc, s: (tile(s), 0, RNN_WIDTH // RG_CB + kc)),
            pl.BlockSpec((ts, bsz, RG_CB), lambda kc, s: (tile(s), 0, kc)),
        ]
        args += [uf3, hf3]
    return pl.pallas_call(
        functools.partial(_rglru_kernel, rev=rev, n_lat=n_lat, n_ctx=n_ctx),
        grid=(RNN_WIDTH // RG_CB, n_lat + n_ctx),
        in_specs=in_specs,
        out_specs=pl.BlockSpec((ts, bsz, RG_CB), lambda kc, s: (tile(s), 0, kc)),
        out_shape=jax.ShapeDtypeStruct((t, bsz, RNN_WIDTH), F32),
        scratch_shapes=[
            pltpu.VMEM((ts + CONV_W - 1, bsz, RG_CB), F32),
            pltpu.VMEM((ts, bsz, RG_CB), F32),
            pltpu.VMEM((ts, bsz, RG_CB), F32),
            pltpu.VMEM((bsz, RG_CB), F32),
        ],
        compiler_params=_cparams(("arbitrary", "arbitrary")),
        name="rglru_rev" if rev else "rglru_fwd",
    )(*args)


ML_HALO = 16


def _mlstm_kernel(*refs, rev, n_lat, n_ctx):
    if rev:
        qc_ref, kc_ref, v_ref, gt_ref, gb_ref, og_ref, ng_ref, hf_ref, o_ref, c_s, n_s, m_s = refs
    else:
        (q_ref, qp_ref, qn_ref, k_ref, kp_ref, kn_ref, v_ref, gt_ref, cw_ref, cb_ref, gb_ref,
         o_ref, qc_ref, kc_ref, c_s, n_s, m_s) = refs
    s = pl.program_id(1)
    L = ML_CHUNK
    tile = _scan_tile(s, n_lat, n_ctx, rev)
    prev_ok, next_ok = _halo_valid(tile, n_lat, n_ctx)

    @pl.when(s == 0)
    def _():
        c_s[...] = jnp.zeros_like(c_s)
        n_s[...] = jnp.zeros_like(n_s)
        m_s[...] = jnp.zeros_like(m_s)

    row = lax.broadcasted_iota(jnp.int32, (L, 1), 0)

    def conv_silu(x_ref, xp_ref, xn_ref, lanes):
        x = x_ref[...].astype(F32)
        xp = xp_ref[ML_HALO - 1:ML_HALO, :].astype(F32) * prev_ok
        xn0 = xn_ref[0:1, :].astype(F32) * next_ok
        xn1 = xn_ref[1:2, :].astype(F32) * next_ok
        x_m1 = jnp.where(row == 0, xp, pltpu.roll(x, 1, 0))
        x_p1 = jnp.where(row == L - 1, xn0, pltpu.roll(x, L - 1, 0))
        x_p2 = jnp.where(row == L - 1, xn1, jnp.where(row == L - 2, xn0, pltpu.roll(x, L - 2, 0)))
        y = (cb_ref[:, lanes] + cw_ref[0:1, lanes] * x_m1 + cw_ref[1:2, lanes] * x
             + cw_ref[2:3, lanes] * x_p1 + cw_ref[3:4, lanes] * x_p2)
        return y * jax.nn.sigmoid(y)

    if rev:
        qc, kc = qc_ref[...], kc_ref[...]
    else:
        qc = conv_silu(q_ref, qp_ref, qn_ref, slice(0, ML_WIDTH)).astype(BF16)
        kc = (conv_silu(k_ref, kp_ref, kn_ref, slice(ML_WIDTH, 2 * ML_WIDTH)) * (ML_DIM ** -0.5)).astype(BF16)
        qc_ref[...], kc_ref[...] = qc, kc
    gates = gt_ref[...] + gb_ref[...]

    ti = lax.broadcasted_iota(jnp.int32, (L, L), 0)
    si = lax.broadcasted_iota(jnp.int32, (L, L), 1)
    causal = (si >= ti) if rev else (si <= ti)
    nt = (((1,), (1,)), ((), ()))
    tn = (((0,), (0,)), ((), ()))
    dsel = 2 * ML_HEADS if rev else 0

    old_states = [(c_s[h], n_s[h], m_s[h][0:1, 0:1]) for h in range(ML_HEADS)]
    new_states = []

    for h in range(ML_HEADS):
        lanes = slice(h * ML_DIM, (h + 1) * ML_DIM)
        q, k, v = qc[:, lanes], kc[:, lanes], v_ref[:, lanes]
        ig = gates[:, dsel + h:dsel + h + 1]
        fg = gates[:, dsel + ML_HEADS + h:dsel + ML_HEADS + h + 1]
        lf = jnp.minimum(fg, 0.0) - jnp.log1p(jnp.exp(-jnp.abs(fg)))
        cmat, nvec, m_prev = old_states[h]

        lf_row = jnp.transpose(jnp.broadcast_to(lf, (L, L)))
        bcum = jnp.sum(jnp.where(causal, lf_row, 0.0), axis=-1, keepdims=True)
        e_row = jnp.transpose(jnp.broadcast_to(bcum - ig, (L, L)))
        dmat = jnp.where(causal, bcum - e_row, -jnp.inf)
        g_inter = bcum + m_prev
        m_t = jnp.maximum(g_inter, jnp.max(dmat, axis=-1, keepdims=True))
        w_inter = jnp.exp(g_inter - m_t)
        smat = lax.dot_general(q, k, nt, preferred_element_type=F32) * jnp.exp(dmat - m_t)
        num =(jnp.dot(smat.astype(BF16), v, preferred_element_type=F32)
               + w_inter * lax.dot_general(q, cmat.astype(BF16), nt, preferred_element_type=F32))
        den = (jnp.sum(smat, axis=-1, keepdims=True)
               + w_inter * jnp.sum(q.astype(F32) * nvec, axis=-1, keepdims=True))
        hout = num / jnp.maximum(jnp.abs(den), jnp.exp(-m_t))

        total = jnp.sum(lf, axis=0, keepdims=True)
        decay = total + m_prev
        w_s = total - bcum + ig
        m_new = jnp.maximum(decay, jnp.max(w_s, axis=0, keepdims=True))
        ws = jnp.exp(w_s - m_new)
        sc = jnp.exp(decay - m_new)
        wv = (ws * v.astype(F32)).astype(BF16)
        new_states.append((sc * cmat + lax.dot_general(wv, k, tn, preferred_element_type=F32),
                           sc * nvec + jnp.sum(ws * k.astype(F32), axis=0, keepdims=True),
                           jnp.broadcast_to(m_new, m_s.shape[1:])))

        if rev:
            hsum = hout + hf_ref[:, lanes]
            ms = jnp.mean(hsum * hsum, axis=-1, keepdims=True)
            hn = hsum * lax.rsqrt(ms + EPS) * ng_ref[:, lanes]
            o_ref[:, lanes] = (hn * jax.nn.sigmoid(og_ref[:, lanes].astype(F32))).astype(o_ref.dtype)
        else:
            o_ref[:, lanes] = hout

    for h, (c_new, n_new, m_new) in enumerate(new_states):
        c_s[h], n_s[h], m_s[h] = c_new, n_new, m_new


def _mlstm(ub, uf, conv_w, conv_b, gate_b, norm_g, l, d, fwd_outs, *, bsz, npb, n_lat, n_ctx):
    t = ub.shape[0]
    L = ML_CHUNK
    rev = d == 1
    tile = functools.partial(_scan_tile, n_lat=n_lat, n_ctx=n_ctx, rev=rev)
    wpb = npb // ML_WIDTH
    hpb = L // ML_HALO

    def main(off):
        return pl.BlockSpec((L, ML_WIDTH), lambda b, s: (tile(s), b * wpb + off // ML_WIDTH))

    def prev(off):
        return pl.BlockSpec((ML_HALO, ML_WIDTH),
                            lambda b, s: (jnp.maximum(tile(s) * hpb - 1, 0), b * wpb + off // ML_WIDTH))

    def nxt(off):
        return pl.BlockSpec((ML_HALO, ML_WIDTH),
                            lambda b, s: (jnp.minimum((tile(s) + 1) * hpb, t // ML_HALO - 1), b * wpb + off // ML_WIDTH))

    per_batch = pl.BlockSpec((L, ML_WIDTH), lambda b, s: (tile(s), b))
    gate_specs = [pl.BlockSpec((L, GATE_PAD), lambda b, s: (tile(s), b)),
                  pl.BlockSpec((None, 1, GATE_PAD), lambda b, s: (l, 0, 0))]
    if rev:
        qc, kc, hf = fwd_outs
        in_specs = [per_batch, per_batch, main(OFF_MV), *gate_specs, main(OFF_MO),
                    pl.BlockSpec((None, 1, ML_WIDTH), lambda b, s: (l, 0, 0)), per_batch]
        args = [qc, kc, ub, uf, gate_b, ub, norm_g, hf]
        out_specs = per_batch
        out_shape = jax.ShapeDtypeStruct((t, bsz * ML_WIDTH), BF16)
    else:
        in_specs = [main(OFF_MQ), prev(OFF_MQ), nxt(OFF_MQ), main(OFF_MK), prev(OFF_MK), nxt(OFF_MK), main(OFF_MV),
                    gate_specs[0],
                    pl.BlockSpec((None, CONV_W, 2 * ML_WIDTH), lambda b, s: (l, 0, 0)),
                    pl.BlockSpec((None, 1, 2 * ML_WIDTH), lambda b, s: (l, 0, 0)),
                    gate_specs[1]]
        args = [ub, ub, ub, ub, ub, ub, ub, uf, conv_w, conv_b, gate_b]
        out_specs = [per_batch, per_batch, per_batch]
        out_shape = [jax.ShapeDtypeStruct((t, bsz * ML_WIDTH), dt) for dt in (F32, BF16, BF16)]
    return pl.pallas_call(
        functools.partial(_mlstm_kernel, rev=rev, n_lat=n_lat, n_ctx=n_ctx),
        grid=(bsz, n_lat + n_ctx),
        in_specs=in_specs,
        out_specs=out_specs,
        out_shape=out_shape,
        scratch_shapes=[
            pltpu.VMEM((ML_HEADS, ML_DIM, ML_DIM), F32),
            pltpu.VMEM((ML_HEADS, 1, ML_DIM), F32),
            pltpu.VMEM((ML_HEADS, SUBLANE, LANE), F32),
        ],
        compiler_params=_cparams(("arbitrary", "arbitrary")),
        name="mlstm_rev" if rev else "mlstm_fwd",
    )(*args)


def _rope_tables(seq, ctx_len):
    rows = seq // GRID_W
    row = jnp.repeat(jnp.arange(rows, dtype=F32), GRID_W)
    col = jnp.tile(jnp.arange(GRID_W, dtype=F32), rows)
    quarter = DA_DIM // 4
    inv = ROPE_BASE ** (-jnp.arange(quarter, dtype=F32) / quarter)
    ar, ac = row[:, None] * inv, col[:, None] * inv
    ang = jnp.concatenate([ar, ac, ar, ac], axis=-1)
    cos, sin = jnp.cos(ang), jnp.sin(ang)
    sin_signed = jnp.where(jnp.arange(DA_DIM) < DA_DIM // 2, -sin, sin)
    pad = lambda tbl, fill: jnp.concatenate([tbl, jnp.full((ctx_len, DA_DIM), fill, F32)], axis=0)
    return pad(cos, 1.0), pad(sin_signed, 0.0)


def _rope_lane_order(x):
    quarter = DA_DIM // 4
    g = x.reshape(*x.shape[:-1], x.shape[-1] // DA_DIM, 4, quarter)
    return g[..., jnp.array([0, 2, 1, 3]), :].reshape(x.shape)


def _row_tile(t, parts, seq):
    tm = t // parts
    assert tm * parts == t and tm % ROW_CHUNK == 0 and seq % ROW_CHUNK == 0, (t, parts)
    return tm


def kernel(x, c, ctx, c_ctx, w_mod, b_mod, norm1_g, norm2_g, w_in, attn_qnorm_g, attn_knorm_g, attn_lambda,
           attn_subln_g, rnn_conv_w, rnn_conv_b, rnn_wa, rnn_ba, rnn_wx, rnn_bx, rnn_lambda, ml_conv_w, ml_conv_b,
           ml_gate_b, ml_norm_g, w_branch_attn, w_branch_rnn, w_branch_ml, w_out, w_ffn1, w_ffn3, w_ffn2):
    bsz, seq, d = x.shape
    ctx_len = ctx.shape[1]
    depth = w_in.shape[0]
    t = seq + ctx_len
    assert bsz == SUBLANE and seq % 256 == 0 and ctx_len % 256 == 0 and d % TN_PROJ == 0

    cuts = [0]
    for w in (DA_WIDTH, DA_WIDTH, DA_WIDTH, RNN_WIDTH, RNN_WIDTH, ML_WIDTH, ML_WIDTH, ML_WIDTH, ML_WIDTH, ML_GATES,
              N_BRANCH * d):
        cuts.append(cuts[-1] + w)
    seg = lambda k: w_in[:, :, cuts[k]:cuts[k + 1]]
    w_b = jnp.concatenate([_rope_lane_order(seg(0)), _rope_lane_order(seg(1)), seg(2), seg(5), seg(6), seg(7), seg(8),
                           seg(10), seg(3), seg(4)], axis=-1).astype(BF16)
    npb = w_b.shape[-1] - 2 * RNN_WIDTH
    w_mg = jnp.pad(seg(9), ((0, 0), (0, 0), (0, GATE_PAD - ML_GATES))).astype(BF16)
    assert npb % ML_WIDTH == 0
    qkg = _rope_lane_order(jnp.stack([attn_qnorm_g, attn_knorm_g], axis=1))
    rope = _rope_tables(seq, ctx_len)
    w_gate = jnp.concatenate([rnn_wa, rnn_wx], axis=-1).astype(BF16)
    rg_bias = jnp.concatenate([rnn_ba[:, :, None], rnn_bx[:, :, None], rnn_lambda[:, :, None],
                               jnp.zeros((depth, 2, SUBLANE - 3, RNN_WIDTH), F32)], axis=2)
    gate_b = jnp.pad(ml_gate_b, ((0, 0), (0, GATE_PAD - ML_GATES)))[:, None, :]
    wba, wbr, wbm, wo = (w.astype(BF16) for w in (w_branch_attn, w_branch_rnn, w_branch_ml, w_out))
    w1, w3, w2 = (w.astype(BF16) for w in (w_ffn1, w_ffn3, w_ffn2))
    row3 = lambda a: a[:, None, :]

    cvecs = jnp.concatenate([c, c_ctx[None, :], jnp.zeros((2 * SUBLANE - bsz - 1, d), F32)], axis=0)
    mod_all = _modvec(cvecs, w_mod, b_mod)[:, :bsz + 1]
    mod_all = jnp.pad(mod_all.reshape(depth, bsz + 1, N_MOD, d), ((0, 0), (0, 0), (0, MOD_ROWS - N_MOD), (0, 0)))

    h = jnp.concatenate([x, ctx], axis=1)
    tm_proj = _row_tile(t, 4, seq)
    n_lat_rg, n_ctx_rg = seq // RG_TS, ctx_len // RG_TS
    n_lat_ml, n_ctx_ml = seq // ML_CHUNK, ctx_len // ML_CHUNK

    for l in range(depth):
        lam_init = 0.8 - 0.6 * math.exp(-0.3 * l)
        mod = mod_all[l]
        ub, ur, uf = _inproj(h, mod, row3(norm1_g), w_b, w_mg, l, qkg, rope, seq=seq, tm=tm_proj, tn=TN_PROJ)
        a = _attention(ub, attn_lambda, row3(attn_subln_g), l, bsz=bsz, seq=seq, npb=npb, tq=256, lam_init=lam_init)
        uf3 = ur.reshape(t, bsz, 2 * RNN_WIDTH).astype(F32)
        hf3 = _rglru(uf3, rnn_conv_w, row3(rnn_conv_b), w_gate, rg_bias, l, 0, None, n_lat=n_lat_rg, n_ctx=n_ctx_rg)
        r = _rglru(uf3, rnn_conv_w, row3(rnn_conv_b), w_gate, rg_bias, l, 1, hf3, n_lat=n_lat_rg, n_ctx=n_ctx_rg)
        r = r.astype(BF16).reshape(t, bsz * RNN_WIDTH)
        hf, qc, kc = _mlstm(ub, uf, ml_conv_w, row3(ml_conv_b), gate_b, row3(ml_norm_g), l, 0, None, bsz=bsz,
                            npb=npb, n_lat=n_lat_ml, n_ctx=n_ctx_ml)
        m = _mlstm(ub, uf, ml_conv_w, row3(ml_conv_b), gate_b, row3(ml_norm_g), l, 1, (qc, kc, hf), bsz=bsz,
                   npb=npb, n_lat=n_lat_ml, n_ctx=n_ctx_ml)
        last = l == depth - 1
        rows, tm = (seq, seq // 4) if last else (t, tm_proj)
        h = _merge(h, mod, a, r, m, ub, wba, wbr, wbm, wo, l, seq=seq, npb=npb, rows=rows, tm=tm // 2, tz=TN_PROJ)
        h = _ffn(h, mod, row3(norm2_g), w1, w3, w2, l, seq=seq, tm=tm, tf=TN_PROJ, tn=TN_PROJ)
    return h
```
